```python
import jax, jax.numpy as jnp
from jax import lax
import numpy as np

D_MODEL = 4096
BATCH = 2
SEQ = 4096
DEPTH = 2

N_HEADS_SB = 16
HEAD_DIM_SB = 128
N_HEADS_RET = 16
HEAD_DIM_RET = 128
HEAD_DIM_RET_V = 128
SB_WIDTH = N_HEADS_SB * HEAD_DIM_SB
RET_WIDTH = N_HEADS_RET * HEAD_DIM_RET
RET_WIDTH_V = N_HEADS_RET * HEAD_DIM_RET_V
D_FF = 11008
ADA_RANK = 256
N_SUBLAYERS = 3
N_MOD = 3 * N_SUBLAYERS
BLOCK_Q = 128
RET_CHUNK = 128
ROPE_BASE = 10000.0
EPS = 1e-6
IN_COLS = 3 * SB_WIDTH + 2 * RET_WIDTH + 2 * RET_WIDTH_V + 2 * D_MODEL

kernel_name = "hybrid_stickbreaking_retention_macaron_block"


def rms_norm(x, g):
    xf = x.astype(jnp.float32)
    y = xf * lax.rsqrt(jnp.mean(xf * xf, axis=-1, keepdims=True) + EPS)
    return (y * g.astype(jnp.float32)).astype(x.dtype)


def modulate(h, shift, scale):
    return h * (1.0 + scale[:, None, :]) + shift[:, None, :]


def swiglu_ffn(h, w_in, w_out):
    a, b = jnp.split(h @ w_in, 2, axis=-1)
    return (jax.nn.silu(a) * b) @ w_out


def split_heads(t, n_heads):
    b, s, w = t.shape
    return t.reshape(b, s, n_heads, w // n_heads).transpose(0, 2, 1, 3)


def merge_heads(t):
    b, h, s, d = t.shape
    return t.transpose(0, 2, 1, 3).reshape(b, s, h * d)


def stick_breaking_attention(q, k, v):
    b, h, s, d = q.shape
    scale = d ** -0.5
    outs = []
    for i in range(s // BLOCK_Q):
        start, end = i * BLOCK_Q, (i + 1) * BLOCK_Q
        qb = q[:, :, start:end].astype(jnp.float32)
        kp = k[:, :, :end].astype(jnp.float32)
        vp = v[:, :, :end].astype(jnp.float32)
        z = jnp.einsum('bhqd,bhkd->bhqk', qb, kp) * scale
        t_pos = start + jnp.arange(BLOCK_Q)[:, None]
        s_pos = jnp.arange(end)[None, :]
        strict = s_pos < t_pos
        log_keep = jnp.where(strict, jax.nn.log_sigmoid(-z), 0.0)
        between = lax.cumsum(log_keep, axis=log_keep.ndim - 1, reverse=True) - log_keep
        weights = jnp.where(strict, jnp.exp(jax.nn.log_sigmoid(z) + between), 0.0)
        outs.append(jnp.einsum('bhqk,bhkd->bhqd', weights, vp).astype(v.dtype))
    return jnp.concatenate(outs, axis=2)


def rotary(x, pos):
    d = x.shape[-1]
    half = d // 2
    inv_freq = ROPE_BASE ** (-jnp.arange(half, dtype=jnp.float32) / half)
    ang = pos.astype(jnp.float32)[:, None] * inv_freq[None, :]
    cos, sin = jnp.cos(ang), jnp.sin(ang)
    x1, x2 = jnp.split(x.astype(jnp.float32), 2, axis=-1)
    return jnp.concatenate([x1 * cos - x2 * sin, x1 * sin + x2 * cos], axis=-1).astype(x.dtype)


def retention_chunkwise(q, k, v):
    b, h, s, d = q.shape
    dv = v.shape[-1]
    c = RET_CHUNK
    n = s // c
    log_gamma = jnp.log1p(-jnp.exp2(-5.0 - jnp.arange(h, dtype=jnp.float32)))
    idx = jnp.arange(c, dtype=jnp.float32)
    qf = q.astype(jnp.float32).reshape(b, h, n, c, d)
    kf = k.astype(jnp.float32).reshape(b, h, n, c, d) * (d ** -0.5)
    vf = v.astype(jnp.float32).reshape(b, h, n, c, dv)
    diff = idx[:, None] - idx[None, :]
    decay = jnp.where(diff >= 0, jnp.exp(log_gamma[:, None, None] * jnp.maximum(diff, 0.0)), 0.0)
    scores = jnp.einsum('bhnid,bhnjd->bhnij', qf, kf) * decay[None, :, None]
    out_inner = jnp.einsum('bhnij,bhnjv->bhniv', scores, vf)
    q_decay = jnp.exp(log_gamma[:, None] * (idx + 1.0))
    k_decay = jnp.exp(log_gamma[:, None] * (c - 1.0 - idx))
    chunk_decay = jnp.exp(log_gamma * c)[None, :, None, None]
    kv = jnp.einsum('bhnjd,bhnjv->nbhdv', kf * k_decay[None, :, None, :, None], vf)

    def step(state, kv_chunk):
        return chunk_decay * state + kv_chunk, state

    _, prev_states = lax.scan(step, jnp.zeros((b, h, d, dv), jnp.float32), kv)
    out_cross = jnp.einsum('bhnid,nbhdv->bhniv', qf * q_decay[None, :, None, :, None], prev_states)
    return (out_inner + out_cross).reshape(b, h, s, dv)


def head_group_norm(o, g):
    mu = jnp.mean(o, axis=-1, keepdims=True)
    var = jnp.mean(jnp.square(o - mu), axis=-1, keepdims=True)
    return merge_heads((o - mu) * lax.rsqrt(var + EPS)) * g.astype(jnp.float32)


def token_mixer(h, pos, w_in, ret_gn, w_branch_a, w_branch_b, w_out):
    sizes = [SB_WIDTH] * 3 + [RET_WIDTH] * 2 + [RET_WIDTH_V] * 2 + [D_MODEL] * 2
    points = [int(p) for p in np.cumsum(sizes)[:-1]]
    qa, ka, va, qb, kb, vb, gb, gate_a, gate_b = jnp.split(h @ w_in, points, axis=-1)
    o_a = stick_breaking_attention(split_heads(qa, N_HEADS_SB), split_heads(ka, N_HEADS_SB), split_heads(va, N_HEADS_SB))
    o_a = merge_heads(o_a) @ w_branch_a
    ret = retention_chunkwise(rotary(split_heads(qb, N_HEADS_RET), pos), rotary(split_heads(kb, N_HEADS_RET), pos),
                              split_heads(vb, N_HEADS_RET))
    o_b = (jax.nn.silu(gb.astype(jnp.float32)) * head_group_norm(ret, ret_gn)).astype(h.dtype) @ w_branch_b
    merged = jax.nn.sigmoid(gate_a) * o_a + jax.nn.sigmoid(gate_b) * o_b
    return merged @ w_out


def setup_inputs(seed: int = 0) -> dict:
    key = jax.random.key(seed)
    ks = jax.random.split(key, 20)
    f32 = jnp.float32

    def nrm(k, shape, scale):
        return jax.random.normal(k, shape, f32) * scale

    return {
        "x": nrm(ks[0], (BATCH, SEQ, D_MODEL), 1.0),
        "c": nrm(ks[1], (BATCH, D_MODEL), 1.0),
        "ada_down": nrm(ks[2], (DEPTH, D_MODEL, ADA_RANK), D_MODEL ** -0.5),
        "ada_up": nrm(ks[3], (DEPTH, ADA_RANK, N_MOD * D_MODEL), 0.5 * ADA_RANK ** -0.5),
        "ada_bias": nrm(ks[4], (DEPTH, N_MOD * D_MODEL), 0.02),
        "norm_ffn1": 1.0 + nrm(ks[5], (DEPTH, D_MODEL), 0.02),
        "ffn1_in": nrm(ks[6], (DEPTH, D_MODEL, 2 * D_FF), D_MODEL ** -0.5),
        "ffn1_out": nrm(ks[7], (DEPTH, D_FF, D_MODEL), D_FF ** -0.5),
        "norm_mix": 1.0 + nrm(ks[8], (DEPTH, D_MODEL), 0.02),
        "w_in": nrm(ks[9], (DEPTH, D_MODEL, IN_COLS), D_MODEL ** -0.5),
        "ret_gn": 1.0 + nrm(ks[10], (DEPTH, RET_WIDTH_V), 0.02),
        "w_branch_a": nrm(ks[11], (DEPTH, SB_WIDTH, D_MODEL), SB_WIDTH ** -0.5),
        "w_branch_b": nrm(ks[12], (DEPTH, RET_WIDTH_V, D_MODEL), RET_WIDTH_V ** -0.5),
        "w_out": nrm(ks[13], (DEPTH, D_MODEL, D_MODEL), D_MODEL ** -0.5),
        "norm_ffn2": 1.0 + nrm(ks[14], (DEPTH, D_MODEL), 0.02),
        "ffn2_in": nrm(ks[15], (DEPTH, D_MODEL, 2 * D_FF), D_MODEL ** -0.5),
        "ffn2_out": nrm(ks[16], (DEPTH, D_FF, D_MODEL), D_FF ** -0.5),
        "norm_final": 1.0 + nrm(ks[17], (D_MODEL,), 0.02),
    }


def reference(x, c, ada_down, ada_up, ada_bias, norm_ffn1, ffn1_in, ffn1_out, norm_mix, w_in, ret_gn,
              w_branch_a, w_branch_b, w_out, norm_ffn2, ffn2_in, ffn2_out, norm_final):
    pos = jnp.arange(x.shape[1], dtype=jnp.int32)
    c_act = jax.nn.silu(c)
    for l in range(DEPTH):
        mod = (c_act @ ada_down[l]) @ ada_up[l] + ada_bias[l]
        sh1, sc1, g1, sh2, sc2, g2, sh3, sc3, g3 = jnp.split(mod, N_MOD, axis=-1)
        h = modulate(rms_norm(x, norm_ffn1[l]), sh1, sc1)
        x = x + 0.5 * g1[:, None, :] * swiglu_ffn(h, ffn1_in[l], ffn1_out[l])
        h = modulate(rms_norm(x, norm_mix[l]), sh2, sc2)
        x = x + g2[:, None, :] * token_mixer(h, pos, w_in[l], ret_gn[l], w_branch_a[l], w_branch_b[l], w_out[l])
        h = modulate(rms_norm(x, norm_ffn2[l]), sh3, sc3)
        x = x + 0.5 * g3[:, None, :] * swiglu_ffn(h, ffn2_in[l], ffn2_out[l])
    return rms_norm(x, norm_final)
```

```python
import functools
import math

import jax
import jax.numpy as jnp
from jax import lax
from jax.experimental import pallas as pl
from jax.experimental.pallas import tpu as pltpu

BF16 = jnp.bfloat16
F32 = jnp.float32

LANES = 128
MXU_DIM = 256
VMEM_LIMIT_BYTES = 56 * 1024 * 1024

HEAD_DIM = 128
N_HEADS = 16
N_MOD = 9
RET_CHUNK = 128
ROPE_BASE = 10000.0
EPS = 1e-6


def _params(*semantics):
    return pltpu.CompilerParams(dimension_semantics=semantics,
                                vmem_limit_bytes=VMEM_LIMIT_BYTES)


def _bdot(a, b):
    return jnp.dot(a, b, preferred_element_type=F32)


def _ada_kernel(c_ref, down_ref, up_ref, bias_ref, o_ref):
    c = c_ref[...]
    ca = c * jax.nn.sigmoid(c)
    t = jnp.dot(ca, down_ref[...], preferred_element_type=F32,
                precision=lax.Precision.HIGHEST)
    o_ref[...] = jnp.dot(t, up_ref[...], preferred_element_type=F32,
                         precision=lax.Precision.HIGHEST) + bias_ref[...]


def _ada_modulation(c, ada_down, ada_up, ada_bias):
    depth, d, r = ada_down.shape
    n = ada_up.shape[2]
    b = c.shape[0]
    rows = 8
    c_pad = jnp.zeros((rows, d), F32).at[:b].set(c)
    tn = min(n, 4096)
    out = pl.pallas_call(
        _ada_kernel,
        grid=(depth, n // tn),
        in_specs=[
            pl.BlockSpec((rows, d), lambda l, j: (0, 0)),
            pl.BlockSpec((None, d, r), lambda l, j: (l, 0, 0)),
            pl.BlockSpec((None, r, tn), lambda l, j: (l, 0, j)),
            pl.BlockSpec((None, 1, tn), lambda l, j: (l, 0, j)),
        ],
        out_specs=pl.BlockSpec((None, rows, tn), lambda l, j: (l, 0, j)),
        out_shape=jax.ShapeDtypeStruct((depth, rows, n), F32),
        compiler_params=_params("arbitrary", "arbitrary"),
        name="ada_modulation",
    )(c_pad, ada_down, ada_up, ada_bias.reshape(depth, 1, n))
    return out[:, :b].reshape(depth, b * N_MOD, 1, d)


def _norm_kernel(x_ref, g_ref, *rest, modulated):
    o_ref = rest[-1]
    x = x_ref[...]
    y = x * lax.rsqrt(jnp.mean(x * x, axis=-1, keepdims=True) + EPS) * g_ref[...]
    if modulated:
        sh_ref, sc_ref = rest[0], rest[1]
        y = y * (1.0 + sc_ref[...]) + sh_ref[...]
    o_ref[...] = y.astype(o_ref.dtype)


def _rms_norm(x, g, seq, mod=None, shift_idx=0, scale_idx=0, out_dtype=BF16):
    m, d = x.shape
    tm = min(512, seq)
    in_specs = [pl.BlockSpec((tm, d), lambda i: (i, 0)),
                pl.BlockSpec((1, d), lambda i: (0, 0))]
    args = [x, g.reshape(1, d)]
    if mod is not None:
        def mod_spec(idx):
            return pl.BlockSpec((None, 1, d),
                                lambda i: ((i * tm // seq) * N_MOD + idx, 0, 0))
        in_specs += [mod_spec(shift_idx), mod_spec(scale_idx)]
        args += [mod, mod]
    return pl.pallas_call(
        functools.partial(_norm_kernel, modulated=mod is not None),
        grid=(m // tm,),
        in_specs=in_specs,
        out_specs=pl.BlockSpec((tm, d), lambda i: (i, 0)),
        out_shape=jax.ShapeDtypeStruct((m, d), out_dtype),
        compiler_params=_params("arbitrary"),
        name="rms_norm",
    )(*args)


def _ffn_in_kernel(h_ref, wa_ref, wb_ref, o_ref):
    h = h_ref[...]
    a = _bdot(h, wa_ref[...].astype(BF16))
    b = _bdot(h, wb_ref[...].astype(BF16))
    o_ref[...] = (a * jax.nn.sigmoid(a) * b).astype(o_ref.dtype)


def _ffn_in(h, w_in, tm):
    m, d = h.shape
    f = w_in.shape[1] // 2
    tn = MXU_DIM
    nb = f // tn
    return pl.pallas_call(
        _ffn_in_kernel,
        grid=(m // tm, nb),
        in_specs=[
            pl.BlockSpec((tm, d), lambda i, j: (i, 0)),
            pl.BlockSpec((d, tn), lambda i, j: (0, j)),
            pl.BlockSpec((d, tn), lambda i, j: (0, j + nb)),
        ],
        out_specs=pl.BlockSpec((tm, tn), lambda i, j: (i, j)),
        out_shape=jax.ShapeDtypeStruct((m, f), BF16),
        compiler_params=_params("arbitrary", "arbitrary"),
        name="ffn_in",
    )(h, w_in, w_in)


def _resid_kernel(a_ref, w_ref, x_ref, g_ref, o_ref, *, coef):
    y = _bdot(a_ref[...], w_ref[...].astype(BF16))
    o_ref[...] = x_ref[...] + (coef * g_ref[...]) * y


def _resid_matmul(a, w, x, mod, gate_idx, coef, seq, tm, tn):
    m, k = a.shape
    n = w.shape[1]
    tn = min(tn, n)
    return pl.pallas_call(
        functools.partial(_resid_kernel, coef=coef),
        grid=(m // tm, n // tn),
        in_specs=[
            pl.BlockSpec((tm, k), lambda i, j: (i, 0)),
            pl.BlockSpec((k, tn), lambda i, j: (0, j)),
            pl.BlockSpec((tm, tn), lambda i, j: (i, j)),
            pl.BlockSpec((None, 1, tn),
                         lambda i, j: ((i * tm // seq) * N_MOD + gate_idx, 0, j)),
        ],
        out_specs=pl.BlockSpec((tm, tn), lambda i, j: (i, j)),
        out_shape=jax.ShapeDtypeStruct((m, n), F32),
        compiler_params=_params("arbitrary", "arbitrary"),
        name="resid_matmul",
    )(a, w, x, mod)


def _rotate_half_pairs(y, cos, sin_signed):
    parts = []
    for s in range(y.shape[1] // HEAD_DIM):
        blk = y[:, s * HEAD_DIM:(s + 1) * HEAD_DIM]
        parts.append(blk * cos + pltpu.roll(blk, HEAD_DIM // 2, 1) * sin_signed)
    return jnp.concatenate(parts, axis=1)


def _proj_kernel(h_ref, w_ref, *rest, act):
    o_ref = rest[-1]
    y = _bdot(h_ref[...], w_ref[...].astype(BF16))
    if act == "sigmoid":
        y = jax.nn.sigmoid(y)
    elif act == "rotary":
        y = _rotate_half_pairs(y, rest[0][...], rest[1][...])
    o_ref[...] = y.astype(o_ref.dtype)


def _proj(h, w, col0, ncols, act, seq, tm, tn, rope=None):
    m, d = h.shape
    j0 = col0 // tn
    in_specs = [pl.BlockSpec((tm, d), lambda i, j: (i, 0)),
                pl.BlockSpec((d, tn), lambda i, j: (0, j + j0))]
    args = [h, w]
    if act == "rotary":
        spt = seq // tm
        in_specs += [pl.BlockSpec((tm, HEAD_DIM), lambda i, j: (i % spt, 0))] * 2
        args += list(rope)
    return pl.pallas_call(
        functools.partial(_proj_kernel, act=act),
        grid=(m // tm, ncols // tn),
        in_specs=in_specs,
        out_specs=pl.BlockSpec((tm, tn), lambda i, j: (i, j)),
        out_shape=jax.ShapeDtypeStruct((m, ncols), BF16),
        compiler_params=_params("arbitrary", "arbitrary"),
        name="proj_" + act,
    )(*args)


def _branch_kernel(a_ref, b_ref, wa_ref, wb_ref, ga_ref, gb_ref, o_ref):
    oa = _bdot(a_ref[...], wa_ref[...].astype(BF16))
    ob = _bdot(b_ref[...], wb_ref[...].astype(BF16))
    o_ref[...] = (ga_ref[...].astype(F32) * oa + gb_ref[...].astype(F32) * ob).astype(o_ref.dtype)


def _branch_merge(attn, retg, w_a, w_b, gates, tm, tn):
    m, ka = attn.shape
    kb = retg.shape[1]
    n = w_a.shape[1]
    tn = min(tn, n)
    nb = n // tn
    return pl.pallas_call(
        _branch_kernel,
        grid=(m // tm, nb),
        in_specs=[
            pl.BlockSpec((tm, ka), lambda i, j: (i, 0)),
            pl.BlockSpec((tm, kb), lambda i, j: (i, 0)),
            pl.BlockSpec((ka, tn), lambda i, j: (0, j)),
            pl.BlockSpec((kb, tn), lambda i, j: (0, j)),
            pl.BlockSpec((tm, tn), lambda i, j: (i, j)),
            pl.BlockSpec((tm, tn), lambda i, j: (i, j + nb)),
        ],
        out_specs=pl.BlockSpec((tm, tn), lambda i, j: (i, j)),
        out_shape=jax.ShapeDtypeStruct((m, n), BF16),
        compiler_params=_params("arbitrary", "arbitrary"),
        name="branch_merge",
    )(attn, retg, w_a, w_b, gates, gates)


def _sb_kernel(q_ref, k_ref, v_ref, o_ref, *, tq, scale):
    qi = pl.program_id(2)
    q = q_ref[...]
    row = lax.broadcasted_iota(jnp.int32, (tq, tq), 0)
    col = lax.broadcasted_iota(jnp.int32, (tq, tq), 1)
    strict = col < row
    suffix = (row > col).astype(BF16)
    suffix2 = jnp.concatenate([suffix, suffix], axis=0)

    def tile(kb, carry, acc, diagonal):
        start = pl.multiple_of(kb * tq, tq)
        k = k_ref[pl.ds(start, tq), :]
        v = v_ref[pl.ds(start, tq), :]
        z = lax.dot_general(q, k, (((1,), (1,)), ((), ())),
                            preferred_element_type=F32) * scale
        softplus = jnp.maximum(z, 0.0) + jnp.log1p(jnp.exp(-jnp.abs(z)))
        log_keep = -softplus
        if diagonal:
            log_keep = jnp.where(strict, log_keep, 0.0)
        hi = log_keep.astype(BF16)
        lo = (log_keep - hi.astype(F32)).astype(BF16)
        between = _bdot(jnp.concatenate([hi, lo], axis=1), suffix2) + carry
        w = jnp.exp((z - softplus) + between)
        if diagonal:
            w = jnp.where(strict, w, 0.0)
        acc = acc + _bdot(w.astype(BF16), v)
        carry = carry + jnp.sum(log_keep, axis=1, keepdims=True)
        return carry, acc

    carry, acc = tile(qi, jnp.zeros((tq, 1), F32), jnp.zeros((tq, HEAD_DIM), F32), True)

    def body(i, state):
        return tile(qi - 1 - i, state[0], state[1], False)

    carry, acc = lax.fori_loop(0, qi, body, (carry, acc))
    o_ref[...] = acc.astype(o_ref.dtype)


def _stick_breaking(qkv, batch, seq, tq):
    m = qkv.shape[0]
    nq = seq // tq
    return pl.pallas_call(
        functools.partial(_sb_kernel, tq=tq, scale=HEAD_DIM ** -0.5),
        grid=(batch, N_HEADS, nq),
        in_specs=[
            pl.BlockSpec((tq, HEAD_DIM), lambda b, h, i: (b * nq + i, h)),
            pl.BlockSpec((seq, HEAD_DIM), lambda b, h, i: (b, N_HEADS + h)),
            pl.BlockSpec((seq, HEAD_DIM), lambda b, h, i: (b, 2 * N_HEADS + h)),
        ],
        out_specs=pl.BlockSpec((tq, HEAD_DIM), lambda b, h, i: (b * nq + i, h)),
        out_shape=jax.ShapeDtypeStruct((m, N_HEADS * HEAD_DIM), BF16),
        compiler_params=_params("arbitrary", "arbitrary", "arbitrary"),
        name="stick_breaking",
    )(qkv, qkv, qkv)


def _ret_kernel(lg_ref, q_ref, k_ref, v_ref, gate_ref, gn_ref, o_ref, state_ref,
                *, chunk, n_chunks, scale):
    @pl.when(pl.program_id(2) == 0)
    def _():
        state_ref[...] = jnp.zeros_like(state_ref)

    lg = lg_ref[pl.program_id(1)]
    c = chunk
    row = lax.broadcasted_iota(jnp.int32, (c, c), 0).astype(F32)
    col = lax.broadcasted_iota(jnp.int32, (c, c), 1).astype(F32)
    diff = row - col
    decay = jnp.where(diff >= 0, jnp.exp(lg * jnp.maximum(diff, 0.0)), 0.0) * scale
    pos = lax.broadcasted_iota(jnp.int32, (c, HEAD_DIM), 0).astype(F32)
    q_decay = jnp.exp(lg * (pos + 1.0))
    k_decay = jnp.exp(lg * (c - 1.0 - pos)) * scale
    chunk_decay = jnp.exp(jnp.zeros((HEAD_DIM, HEAD_DIM), F32) + lg * c)
    gn = gn_ref[...]

    state = state_ref[...]
    for n in range(n_chunks):
        rows = slice(n * c, (n + 1) * c)
        q = q_ref[rows, :]
        k = k_ref[rows, :]
        v = v_ref[rows, :]
        scores = lax.dot_general(q, k, (((1,), (1,)), ((), ())),
                                 preferred_element_type=F32) * decay
        out = _bdot(scores.astype(BF16), v)
        out = out + _bdot((q.astype(F32) * q_decay).astype(BF16), state.astype(BF16))
        kv = lax.dot_general((k.astype(F32) * k_decay).astype(BF16), v,
                             (((0,), (0,)), ((), ())), preferred_element_type=F32)
        state = chunk_decay * state + kv
        mu = jnp.mean(out, axis=-1, keepdims=True)
        cen = out - mu
        var = jnp.mean(cen * cen, axis=-1, keepdims=True)
        normed = cen * lax.rsqrt(var + EPS) * gn
        g = gate_ref[rows, :].astype(F32)
        o_ref[rows, :] = (g * jax.nn.sigmoid(g) * normed).astype(o_ref.dtype)
    state_ref[...] = state


def _retention(qk, vg, ret_gn, batch, seq, rows_per_step):
    m = qk.shape[0]
    ns = seq // rows_per_step
    log_gamma = jnp.log1p(-jnp.exp2(-5.0 - jnp.arange(N_HEADS, dtype=F32)))
    blk = (rows_per_step, HEAD_DIM)
    return pl.pallas_call(
        functools.partial(_ret_kernel, chunk=RET_CHUNK, n_chunks=rows_per_step // RET_CHUNK,
                          scale=HEAD_DIM ** -0.5),
        grid=(batch, N_HEADS, ns),
        in_specs=[
            pl.BlockSpec(memory_space=pltpu.SMEM),
            pl.BlockSpec(blk, lambda b, h, s: (b * ns + s, h)),
            pl.BlockSpec(blk, lambda b, h, s: (b * ns + s, N_HEADS + h)),
            pl.BlockSpec(blk, lambda b, h, s: (b * ns + s, h)),
            pl.BlockSpec(blk, lambda b, h, s: (b * ns + s, N_HEADS + h)),
            pl.BlockSpec((None, 1, HEAD_DIM), lambda b, h, s: (h, 0, 0)),
        ],
        out_specs=pl.BlockSpec(blk, lambda b, h, s: (b * ns + s, h)),
        out_shape=jax.ShapeDtypeStruct((m, N_HEADS * HEAD_DIM), BF16),
        scratch_shapes=[pltpu.VMEM((HEAD_DIM, HEAD_DIM), F32)],
        compiler_params=_params("arbitrary", "arbitrary", "arbitrary"),
        name="retention",
    )(log_gamma, qk, qk, vg, vg, ret_gn.reshape(N_HEADS, 1, HEAD_DIM))


def _rope_tables(seq):
    half = HEAD_DIM // 2
    inv_freq = ROPE_BASE ** (-jnp.arange(half, dtype=F32) / half)
    ang = jnp.arange(seq, dtype=jnp.int32).astype(F32)[:, None] * inv_freq[None, :]
    cos, sin = jnp.cos(ang), jnp.sin(ang)
    return (jnp.concatenate([cos, cos], axis=-1), jnp.concatenate([-sin, sin], axis=-1))


def kernel(x, c, ada_down, ada_up, ada_bias, norm_ffn1, ffn1_in, ffn1_out, norm_mix, w_in, ret_gn,
           w_branch_a, w_branch_b, w_out, norm_ffn2, ffn2_in, ffn2_out, norm_final):
    batch, seq, d = x.shape
    depth = ada_down.shape[0]
    width = N_HEADS * HEAD_DIM
    tm = min(1024, seq)
    tq = min(256, seq)
    rope = _rope_tables(seq)
    mod = _ada_modulation(c, ada_down, ada_up, ada_bias)
    xf = x.reshape(batch * seq, d)
    for l in range(depth):
        ml = mod[l]
        h = _rms_norm(xf, norm_ffn1[l], seq, ml, 0, 1)
        xf = _resid_matmul(_ffn_in(h, ffn1_in[l], tm), ffn1_out[l], xf, ml, 2, 0.5, seq,
                           min(512, seq), MXU_DIM)
        h = _rms_norm(xf, norm_mix[l], seq, ml, 3, 4)
        qkv = _proj(h, w_in[l], 0, 3 * width, "none", seq, tm, 512)
        qk = _proj(h, w_in[l], 3 * width, 2 * width, "rotary", seq, tm, 512, rope)
        vg = _proj(h, w_in[l], 5 * width, 2 * width, "none", seq, tm, 512)
        gates = _proj(h, w_in[l], 7 * width, 2 * d, "sigmoid", seq, tm, 512)
        attn = _stick_breaking(qkv, batch, seq, tq)
        retg = _retention(qk, vg, ret_gn[l], batch, seq, min(1024, seq))
        merged = _branch_merge(attn, retg, w_branch_a[l], w_branch_b[l], gates, tm, 512)
        xf = _resid_matmul(merged, w_out[l], xf, ml, 5, 1.0, seq, tm, 512)
        h = _rms_norm(xf, norm_ffn2[l], seq, ml, 6, 7)
        xf = _resid_matmul(_ffn_in(h, ffn2_in[l], tm), ffn2_out[l], xf, ml, 8, 0.5, seq,
                           min(512, seq), MXU_DIM)
    out = _rms_norm(xf, norm_final, seq, out_dtype=x.dtype)
    return out.reshape(batch, seq, d)
```

```python
import functools
import math

import jax
import jax.numpy as jnp
from jax import lax
from jax.experimental import pallas as pl
from jax.experimental.pallas import tpu as pltpu

BF16 = jnp.bfloat16
F32 = jnp.float32

LANES = 128
MXU_DIM = 256
VMEM_LIMIT_BYTES = 56 * 1024 * 1024

HEAD_DIM = 128
N_HEADS = 16
N_MOD = 9
RET_CHUNK = 128
ROPE_BASE = 10000.0
EPS = 1e-6
LOG2_E = math.log2(math.e)


def _params(*semantics):
    return pltpu.CompilerParams(dimension_semantics=semantics,
                                vmem_limit_bytes=VMEM_LIMIT_BYTES)


def _bdot(a, b):
    return jnp.dot(a, b, preferred_element_type=F32)


def _panel_spec(tm, k):
    return pl.BlockSpec((tm, k), lambda i, j: (i, 0), pipeline_mode=pl.Buffered(1))


def _ada_kernel(c_ref, down_ref, up_ref, bias_ref, o_ref):
    c = c_ref[...]
    ca = c * jax.nn.sigmoid(c)
    t = jnp.dot(ca, down_ref[...], preferred_element_type=F32,
                precision=lax.Precision.HIGHEST)
    o_ref[...] = jnp.dot(t, up_ref[...], preferred_element_type=F32,
                         precision=lax.Precision.HIGHEST) + bias_ref[...]


def _ada_modulation(c, ada_down, ada_up, ada_bias):
    depth, d, r = ada_down.shape
    n = ada_up.shape[2]
    b = c.shape[0]
    rows = 8
    c_pad = jnp.zeros((rows, d), F32).at[:b].set(c)
    tn = min(n, 4096)
    out = pl.pallas_call(
        _ada_kernel,
        grid=(depth, n // tn),
        in_specs=[
            pl.BlockSpec((rows, d), lambda l, j: (0, 0)),
            pl.BlockSpec((None, d, r), lambda l, j: (l, 0, 0)),
            pl.BlockSpec((None, r, tn), lambda l, j: (l, 0, j)),
            pl.BlockSpec((None, 1, tn), lambda l, j: (l, 0, j)),
        ],
        out_specs=pl.BlockSpec((None, rows, tn), lambda l, j: (l, 0, j)),
        out_shape=jax.ShapeDtypeStruct((depth, rows, n), F32),
        compiler_params=_params("arbitrary", "arbitrary"),
        name="ada_modulation",
    )(c_pad, ada_down, ada_up, ada_bias.reshape(depth, 1, n))
    return out[:, :b].reshape(depth * b * N_MOD, 1, d)


class _Mod:
    def __init__(self, table, layer, batch, seq):
        self.table, self.base, self.seq = table, layer * batch * N_MOD, seq

    def row(self, first_token, idx):
        return self.base + (first_token // self.seq) * N_MOD + idx


def _norm_kernel(x_ref, g_ref, *rest, modulated):
    o_ref = rest[-1]
    x = x_ref[...]
    y = x * lax.rsqrt(jnp.mean(x * x, axis=-1, keepdims=True) + EPS) * g_ref[...]
    if modulated:
        sh_ref, sc_ref = rest[0], rest[1]
        y = y * (1.0 + sc_ref[...]) + sh_ref[...]
    o_ref[...] = y.astype(o_ref.dtype)


def _rms_norm(x, g, layer, seq, mod=None, shift_idx=0, scale_idx=0, out_dtype=None):
    m, d = x.shape
    tm = min(512, seq)
    in_specs = [pl.BlockSpec((tm, d), lambda i: (i, 0)),
                pl.BlockSpec((None, 1, d), lambda i: (layer, 0, 0))]
    args = [x, g.reshape(-1, 1, d)]
    if mod is not None:
        def mod_spec(idx):
            return pl.BlockSpec((None, 1, d), lambda i: (mod.row(i * tm, idx), 0, 0))
        in_specs += [mod_spec(shift_idx), mod_spec(scale_idx)]
        args += [mod.table, mod.table]
    return pl.pallas_call(
        functools.partial(_norm_kernel, modulated=mod is not None),
        grid=(m // tm,),
        in_specs=in_specs,
        out_specs=pl.BlockSpec((tm, d), lambda i: (i, 0)),
        out_shape=jax.ShapeDtypeStruct((m, d), out_dtype or BF16),
        compiler_params=_params("arbitrary"),
        name="rms_norm",
    )(*args)


def _ffn_in_kernel(h_ref, wa_ref, wb_ref, o_ref):
    h = h_ref[...]
    a = _bdot(h, wa_ref[...].astype(BF16))
    b = _bdot(h, wb_ref[...].astype(BF16))
    o_ref[...] = (a * jax.nn.sigmoid(a) * b).astype(o_ref.dtype)


def _ffn_in(h, w_in, layer, tm):
    m, d = h.shape
    f = w_in.shape[2] // 2
    tn = MXU_DIM
    nb = f // tn
    return pl.pallas_call(
        _ffn_in_kernel,
        grid=(m // tm, nb),
        in_specs=[
            _panel_spec(tm, d),
            pl.BlockSpec((None, d, tn), lambda i, j: (layer, 0, j)),
            pl.BlockSpec((None, d, tn), lambda i, j: (layer, 0, j + nb)),
        ],
        out_specs=pl.BlockSpec((tm, tn), lambda i, j: (i, j)),
        out_shape=jax.ShapeDtypeStruct((m, f), BF16),
        compiler_params=_params("arbitrary", "arbitrary"),
        name="ffn_in",
    )(h, w_in, w_in)


def _resid_kernel(a_ref, w_ref, x_ref, g_ref, o_ref, *, coef):
    y = _bdot(a_ref[...], w_ref[...].astype(BF16))
    o_ref[...] = x_ref[...] + (coef * g_ref[...]) * y


def _resid_matmul(a, w, layer, x, mod, gate_idx, coef, tm, tn):
    m, k = a.shape
    n = w.shape[2]
    tn = min(tn, n)
    return pl.pallas_call(
        functools.partial(_resid_kernel, coef=coef),
        grid=(m // tm, n // tn),
        in_specs=[
            _panel_spec(tm, k),
            pl.BlockSpec((None, k, tn), lambda i, j: (layer, 0, j)),
            pl.BlockSpec((tm, tn), lambda i, j: (i, j)),
            pl.BlockSpec((None, 1, tn), lambda i, j: (mod.row(i * tm, gate_idx), 0, j)),
        ],
        out_specs=pl.BlockSpec((tm, tn), lambda i, j: (i, j)),
        out_shape=jax.ShapeDtypeStruct((m, n), F32),
        compiler_params=_params("arbitrary", "arbitrary"),
        name="resid_matmul",
    )(a, w, x, mod.table)


def _rotate_half_pairs(y, cos, sin_signed):
    parts = []
    for s in range(y.shape[1] // HEAD_DIM):
        blk = y[:, s * HEAD_DIM:(s + 1) * HEAD_DIM]
        parts.append(blk * cos + pltpu.roll(blk, HEAD_DIM // 2, 1) * sin_signed)
    return jnp.concatenate(parts, axis=1)


def _proj_kernel(h_ref, w_ref, *rest, act, n_scaled, scale):
    o_ref = rest[-1]
    y = _bdot(h_ref[...], w_ref[...].astype(BF16))
    if act == "sigmoid":
        y = jax.nn.sigmoid(y)
    elif act == "rotary":
        y = _rotate_half_pairs(y, rest[0][...], rest[1][...])
    elif act == "scale_leading":
        y = y * jnp.where(pl.program_id(1) < n_scaled, scale, 1.0)
    o_ref[...] = y.astype(o_ref.dtype)


def _proj(h, w, layer, col0, ncols, act, seq, tm, tn, rope=None, n_scaled=0, scale=1.0):
    m, d = h.shape
    j0 = col0 // tn
    in_specs = [_panel_spec(tm, d),
                pl.BlockSpec((None, d, tn), lambda i, j: (layer, 0, j + j0))]
    args = [h, w]
    if act == "rotary":
        spt = seq // tm
        in_specs += [pl.BlockSpec((tm, HEAD_DIM), lambda i, j: (i % spt, 0))] * 2
        args += list(rope)
    return pl.pallas_call(
        functools.partial(_proj_kernel, act=act, n_scaled=n_scaled, scale=scale),
        grid=(m // tm, ncols // tn),
        in_specs=in_specs,
        out_specs=pl.BlockSpec((tm, tn), lambda i, j: (i, j)),
        out_shape=jax.ShapeDtypeStruct((m, ncols), BF16),
        compiler_params=_params("arbitrary", "arbitrary"),
        name="proj_" + act,
    )(*args)


def _branch_kernel(a_ref, b_ref, wa_ref, wb_ref, ga_ref, gb_ref, o_ref):
    oa = _bdot(a_ref[...], wa_ref[...].astype(BF16))
    ob = _bdot(b_ref[...], wb_ref[...].astype(BF16))
    o_ref[...] = (ga_ref[...].astype(F32) * oa + gb_ref[...].astype(F32) * ob).astype(o_ref.dtype)


def _branch_merge(attn, retg, w_a, w_b, layer, gates, tm, tn):
    m, ka = attn.shape
    kb = retg.shape[1]
    n = w_a.shape[2]
    tn = min(tn, n)
    nb = n // tn
    return pl.pallas_call(
        _branch_kernel,
        grid=(m // tm, nb),
        in_specs=[
            _panel_spec(tm, ka),
            _panel_spec(tm, kb),
            pl.BlockSpec((None, ka, tn), lambda i, j: (layer, 0, j)),
            pl.BlockSpec((None, kb, tn), lambda i, j: (layer, 0, j)),
            pl.BlockSpec((tm, tn), lambda i, j: (i, j)),
            pl.BlockSpec((tm, tn), lambda i, j: (i, j + nb)),
        ],
        out_specs=pl.BlockSpec((tm, tn), lambda i, j: (i, j)),
        out_shape=jax.ShapeDtypeStruct((m, n), BF16),
        compiler_params=_params("arbitrary", "arbitrary"),
        name="branch_merge",
    )(attn, retg, w_a, w_b, gates, gates)


def _sb_kernel(q_ref, k_ref, v_ref, o_ref, *, tq):
    qi = pl.program_id(2)
    q = q_ref[...]
    row = lax.broadcasted_iota(jnp.int32, (tq, tq), 0)
    col = lax.broadcasted_iota(jnp.int32, (tq, tq), 1)
    strict = col < row
    suffix = jnp.concatenate([(row > col).astype(BF16), jnp.ones((tq, LANES), BF16)], axis=1)
    rhs = jnp.concatenate([suffix, suffix], axis=0)

    def tile(kb, diagonal):
        start = pl.multiple_of(kb * tq, tq)
        k = k_ref[pl.ds(start, tq), :]
        v = v_ref[pl.ds(start, tq), :]
        z = lax.dot_general(q, k, (((1,), (1,)), ((), ())), preferred_element_type=F32)
        log_beta = jnp.minimum(z, 0.0) - jnp.log2(1.0 + jnp.exp2(-jnp.abs(z)))
        log_keep = log_beta - z
        if diagonal:
            log_keep = jnp.where(strict, log_keep, 0.0)
        hi = log_keep.astype(BF16)
        lo = (log_keep - hi.astype(F32)).astype(BF16)
        sums = _bdot(jnp.concatenate([hi, lo], axis=1), rhs)
        w = jnp.exp2(log_beta + sums[:, :tq])
        if diagonal:
            w = jnp.where(strict, w, 0.0)
        return _bdot(w.astype(BF16), v), sums[:, tq:]

    def two_tiles(kb, diagonal, carry, scale, acc):
        pv_a, total_a = tile(kb, diagonal)
        pv_b, total_b = tile(jnp.maximum(kb - 1, 0), False)
        has_b = jnp.where(kb > 0, 1.0, 0.0)
        carry_a = carry + total_a
        acc = acc + scale * pv_a + (jnp.exp2(carry_a) * has_b) * pv_b
        carry = carry_a + has_b * total_b
        return carry, jnp.exp2(carry), acc

    zeros = jnp.zeros((tq, HEAD_DIM), F32)
    carry, scale, acc = two_tiles(qi, True, zeros, jnp.ones((tq, HEAD_DIM), F32), zeros)

    def cond(state):
        kb, _, scale, _ = state
        return jnp.logical_and(kb >= 0, jnp.max(scale) > 0.0)

    def body(state):
        kb, carry, scale, acc = state
        return (kb - 2,) + two_tiles(kb, False, carry, scale, acc)

    _, _, _, acc = lax.while_loop(cond, body, (qi - 2, carry, scale, acc))
    o_ref[...] = acc.astype(o_ref.dtype)


def _stick_breaking(qkv, batch, seq, tq):
    m = qkv.shape[0]
    nq = seq // tq
    return pl.pallas_call(
        functools.partial(_sb_kernel, tq=tq),
        grid=(batch, N_HEADS, nq),
        in_specs=[
            pl.BlockSpec((tq, HEAD_DIM), lambda b, h, i: (b * nq + i, h)),
            pl.BlockSpec((seq, HEAD_DIM), lambda b, h, i: (b, N_HEADS + h)),
            pl.BlockSpec((seq, HEAD_DIM), lambda b, h, i: (b, 2 * N_HEADS + h)),
        ],
        out_specs=pl.BlockSpec((tq, HEAD_DIM), lambda b, h, i: (b * nq + i, h)),
        out_shape=jax.ShapeDtypeStruct((m, N_HEADS * HEAD_DIM), BF16),
        compiler_params=_params("arbitrary", "arbitrary", "arbitrary"),
        name="stick_breaking",
    )(qkv, qkv, qkv)


def _ret_kernel(lg_ref, q_ref, k_ref, v_ref, gate_ref, gn_ref, o_ref, state_ref,
                *, chunk, n_chunks, scale):
    @pl.when(pl.program_id(2) == 0)
    def _():
        state_ref[...] = jnp.zeros_like(state_ref)

    lg = lg_ref[pl.program_id(1)]
    c = chunk
    row = lax.broadcasted_iota(jnp.int32, (c, c), 0).astype(F32)
    col = lax.broadcasted_iota(jnp.int32, (c, c), 1).astype(F32)
    diff = row - col
    decay = jnp.where(diff >= 0, jnp.exp(lg * jnp.maximum(diff, 0.0)), 0.0) * scale
    pos = lax.broadcasted_iota(jnp.int32, (c, HEAD_DIM), 0).astype(F32)
    q_decay = jnp.exp(lg * (pos + 1.0))
    k_decay = jnp.exp(lg * (c - 1.0 - pos)) * scale
    chunk_decay = jnp.exp(jnp.zeros((HEAD_DIM, HEAD_DIM), F32) + lg * c)
    gn = gn_ref[...]

    state = state_ref[...]
    for n in range(n_chunks):
        rows = slice(n * c, (n + 1) * c)
        q = q_ref[rows, :]
        k = k_ref[rows, :]
        v = v_ref[rows, :]
        scores = lax.dot_general(q, k, (((1,), (1,)), ((), ())),
                                 preferred_element_type=F32) * decay
        out = _bdot(scores.astype(BF16), v)
        out = out + _bdot((q.astype(F32) * q_decay).astype(BF16), state.astype(BF16))
        kv = lax.dot_general((k.astype(F32) * k_decay).astype(BF16), v,
                             (((0,), (0,)), ((), ())), preferred_element_type=F32)
        state = chunk_decay * state + kv
        mu = jnp.mean(out, axis=-1, keepdims=True)
        cen = out - mu
        var = jnp.mean(cen * cen, axis=-1, keepdims=True)
        normed = cen * lax.rsqrt(var + EPS) * gn
        g = gate_ref[rows, :].astype(F32)
        o_ref[rows, :] = (g * jax.nn.sigmoid(g) * normed).astype(o_ref.dtype)
    state_ref[...] = state


def _retention(qk, vg, ret_gn, layer, batch, seq, rows_per_step):
    m = qk.shape[0]
    ns = seq // rows_per_step
    log_gamma = jnp.log1p(-jnp.exp2(-5.0 - jnp.arange(N_HEADS, dtype=F32)))
    blk = (rows_per_step, HEAD_DIM)
    return pl.pallas_call(
        functools.partial(_ret_kernel, chunk=RET_CHUNK, n_chunks=rows_per_step // RET_CHUNK,
                          scale=HEAD_DIM ** -0.5),
        grid=(batch, N_HEADS, ns),
        in_specs=[
            pl.BlockSpec(memory_space=pltpu.SMEM),
            pl.BlockSpec(blk, lambda b, h, s: (b * ns + s, h)),
            pl.BlockSpec(blk, lambda b, h, s: (b * ns + s, N_HEADS + h)),
            pl.BlockSpec(blk, lambda b, h, s: (b * ns + s, h)),
            pl.BlockSpec(blk, lambda b, h, s: (b * ns + s, N_HEADS + h)),
            pl.BlockSpec((None, 1, HEAD_DIM), lambda b, h, s: (layer * N_HEADS + h, 0, 0)),
        ],
        out_specs=pl.BlockSpec(blk, lambda b, h, s: (b * ns + s, h)),
        out_shape=jax.ShapeDtypeStruct((m, N_HEADS * HEAD_DIM), BF16),
        scratch_shapes=[pltpu.VMEM((HEAD_DIM, HEAD_DIM), F32)],
        compiler_params=_params("arbitrary", "arbitrary", "arbitrary"),
        name="retention",
    )(log_gamma, qk, qk, vg, vg, ret_gn.reshape(-1, 1, HEAD_DIM))


def _rope_tables(seq):
    half = HEAD_DIM // 2
    inv_freq = ROPE_BASE ** (-jnp.arange(half, dtype=F32) / half)
    ang = jnp.arange(seq, dtype=jnp.int32).astype(F32)[:, None] * inv_freq[None, :]
    cos, sin = jnp.cos(ang), jnp.sin(ang)
    return (jnp.concatenate([cos, cos], axis=-1), jnp.concatenate([-sin, sin], axis=-1))


def kernel(x, c, ada_down, ada_up, ada_bias, norm_ffn1, ffn1_in, ffn1_out, norm_mix, w_in, ret_gn,
           w_branch_a, w_branch_b, w_out, norm_ffn2, ffn2_in, ffn2_out, norm_final):
    batch, seq, d = x.shape
    depth = ada_down.shape[0]
    width = N_HEADS * HEAD_DIM
    tm_wide = min(2048, seq)
    tm_deep = min(1024, seq)
    tq = min(256, seq)
    tn = 512
    rope = _rope_tables(seq)
    table = _ada_modulation(c, ada_down, ada_up, ada_bias)
    xf = x.reshape(batch * seq, d)
    for l in range(depth):
        mod = _Mod(table, l, batch, seq)
        h = _rms_norm(xf, norm_ffn1, l, seq, mod, 0, 1)
        xf = _resid_matmul(_ffn_in(h, ffn1_in, l, tm_wide), ffn1_out, l, xf, mod, 2, 0.5,
                           tm_deep, MXU_DIM)
        h = _rms_norm(xf, norm_mix, l, seq, mod, 3, 4)
        qkv = _proj(h, w_in, l, 0, 3 * width, "scale_leading", seq, tm_wide, tn,
                    n_scaled=width // tn, scale=HEAD_DIM ** -0.5 * LOG2_E)
        qk = _proj(h, w_in, l, 3 * width, 2 * width, "rotary", seq, tm_wide, tn, rope)
        vg = _proj(h, w_in, l, 5 * width, 2 * width, "none", seq, tm_wide, tn)
        gates = _proj(h, w_in, l, 7 * width, 2 * d, "sigmoid", seq, tm_wide, tn)
        attn = _stick_breaking(qkv, batch, seq, tq)
        retg = _retention(qk, vg, ret_gn, l, batch, seq, min(1024, seq))
        merged = _branch_merge(attn, retg, w_branch_a, w_branch_b, l, gates, tm_wide, tn)
        xf = _resid_matmul(merged, w_out, l, xf, mod, 5, 1.0, tm_wide, tn)
        h = _rms_norm(xf, norm_ffn2, l, seq, mod, 6, 7)
        xf = _resid_matmul(_ffn_in(h, ffn2_in, l, tm_wide), ffn2_out, l, xf, mod, 8, 0.5,
                           tm_deep, MXU_DIM)
    out = _rms_norm(xf, norm_final.reshape(1, d), 0, seq, out_dtype=x.dtype)
    return out.reshape(batch, seq, d)
```

```python
import functools
import math

import jax
import jax.numpy as jnp
from jax import lax
from jax.experimental import pallas as pl
from jax.experimental.pallas import tpu as pltpu

BF16 = jnp.bfloat16
F32 = jnp.float32

LANES = 128
MXU_DIM = 256
VMEM_LIMIT_BYTES = 56 * 1024 * 1024

HEAD_DIM = 128
N_HEADS = 16
N_MOD = 9
RET_CHUNK = 128
ROPE_BASE = 10000.0
EPS = 1e-6
LOG2_E = math.log2(math.e)


def _params(*semantics):
    return pltpu.CompilerParams(dimension_semantics=semantics,
                                vmem_limit_bytes=VMEM_LIMIT_BYTES)


def _bdot(a, b):
    return jnp.dot(a, b, preferred_element_type=F32)


def _panel_spec(tm, k):
    return pl.BlockSpec((tm, k), lambda i, j: (i, 0), pipeline_mode=pl.Buffered(1))


def _ada_kernel(c_ref, down_ref, up_ref, bias_ref, o_ref, t_ref):
    @pl.when(pl.program_id(1) == 0)
    def _():
        c = c_ref[...]
        t_ref[...] = jnp.dot(c * jax.nn.sigmoid(c), down_ref[...], preferred_element_type=F32,
                             precision=lax.Precision.HIGHEST)

    o_ref[...] = jnp.dot(t_ref[...], up_ref[...], preferred_element_type=F32,
                         precision=lax.Precision.HIGHEST) + bias_ref[...]


def _ada_modulation(c, ada_down, ada_up, ada_bias):
    depth, d, r = ada_down.shape
    n = ada_up.shape[2]
    b = c.shape[0]
    rows = 8
    c_pad = jnp.zeros((rows, d), F32).at[:b].set(c)
    tn = min(n, 4096)
    out = pl.pallas_call(
        _ada_kernel,
        grid=(depth, n // tn),
        in_specs=[
            pl.BlockSpec((rows, d), lambda l, j: (0, 0)),
            pl.BlockSpec((None, d, r), lambda l, j: (l, 0, 0)),
            pl.BlockSpec((None, r, tn), lambda l, j: (l, 0, j)),
            pl.BlockSpec((None, 1, tn), lambda l, j: (l, 0, j)),
        ],
        out_specs=pl.BlockSpec((None, rows, tn), lambda l, j: (l, 0, j)),
        out_shape=jax.ShapeDtypeStruct((depth, rows, n), F32),
        scratch_shapes=[pltpu.VMEM((rows, r), F32)],
        compiler_params=_params("arbitrary", "arbitrary"),
        name="ada_modulation",
    )(c_pad, ada_down, ada_up, ada_bias.reshape(depth, 1, n))
    return out[:, :b].reshape(depth * b * N_MOD, 1, d)


class _Mod:
    def __init__(self, table, layer, batch, seq):
        self.table, self.base, self.seq = table, layer * batch * N_MOD, seq

    def row(self, first_token, idx):
        return self.base + (first_token // self.seq) * N_MOD + idx


def _norm_kernel(x_ref, g_ref, *rest, modulated):
    o_ref = rest[-1]
    x = x_ref[...]
    y = x * lax.rsqrt(jnp.mean(x * x, axis=-1, keepdims=True) + EPS) * g_ref[...]
    if modulated:
        sh_ref, sc_ref = rest[0], rest[1]
        y = y * (1.0 + sc_ref[...]) + sh_ref[...]
    o_ref[...] = y.astype(o_ref.dtype)


def _rms_norm(x, g, layer, seq, mod=None, shift_idx=0, scale_idx=0, out_dtype=None):
    m, d = x.shape
    tm = min(512, seq)
    in_specs = [pl.BlockSpec((tm, d), lambda i: (i, 0)),
                pl.BlockSpec((None, 1, d), lambda i: (layer, 0, 0))]
    args = [x, g.reshape(-1, 1, d)]
    if mod is not None:
        def mod_spec(idx):
            return pl.BlockSpec((None, 1, d), lambda i: (mod.row(i * tm, idx), 0, 0))
        in_specs += [mod_spec(shift_idx), mod_spec(scale_idx)]
        args += [mod.table, mod.table]
    return pl.pallas_call(
        functools.partial(_norm_kernel, modulated=mod is not None),
        grid=(m // tm,),
        in_specs=in_specs,
        out_specs=pl.BlockSpec((tm, d), lambda i: (i, 0)),
        out_shape=jax.ShapeDtypeStruct((m, d), out_dtype or BF16),
        compiler_params=_params("arbitrary"),
        name="rms_norm",
    )(*args)


EPILOGUE_CHUNK_ROWS = 512


def _row_chunks(tm, rows):
    rows = min(rows, tm)
    return [slice(r, r + rows) for r in range(0, tm, rows)]


def _ffn_in_kernel(h_ref, wa_ref, wb_ref, o_ref):
    wa = wa_ref[...].astype(BF16)
    wb = wb_ref[...].astype(BF16)
    for rows in _row_chunks(h_ref.shape[0], EPILOGUE_CHUNK_ROWS):
        h = h_ref[rows, :]
        a = _bdot(h, wa)
        b = _bdot(h, wb)
        o_ref[rows, :] = (a * jax.nn.sigmoid(a) * b).astype(o_ref.dtype)


def _ffn_in(h, w_in, layer, tm):
    m, d = h.shape
    f = w_in.shape[2] // 2
    tn = MXU_DIM
    nb = f // tn
    return pl.pallas_call(
        _ffn_in_kernel,
        grid=(m // tm, nb),
        in_specs=[
            _panel_spec(tm, d),
            pl.BlockSpec((None, d, tn), lambda i, j: (layer, 0, j)),
            pl.BlockSpec((None, d, tn), lambda i, j: (layer, 0, j + nb)),
        ],
        out_specs=pl.BlockSpec((tm, tn), lambda i, j: (i, j)),
        out_shape=jax.ShapeDtypeStruct((m, f), BF16),
        compiler_params=_params("arbitrary", "arbitrary"),
        name="ffn_in",
    )(h, w_in, w_in)


def _resid_kernel(a_ref, w_ref, x_ref, g_ref, o_ref, *, coef):
    w = w_ref[...].astype(BF16)
    gate = coef * g_ref[...]
    for rows in _row_chunks(a_ref.shape[0], EPILOGUE_CHUNK_ROWS):
        o_ref[rows, :] = x_ref[rows, :] + gate * _bdot(a_ref[rows, :], w)


def _resid_matmul(a, w, layer, x, mod, gate_idx, coef, tm, tn):
    m, k = a.shape
    n = w.shape[2]
    tn = min(tn, n)
    return pl.pallas_call(
        functools.partial(_resid_kernel, coef=coef),
        grid=(m // tm, n // tn),
        in_specs=[
            _panel_spec(tm, k),
            pl.BlockSpec((None, k, tn), lambda i, j: (layer, 0, j)),
            pl.BlockSpec((tm, tn), lambda i, j: (i, j)),
            pl.BlockSpec((None, 1, tn), lambda i, j: (mod.row(i * tm, gate_idx), 0, j)),
        ],
        out_specs=pl.BlockSpec((tm, tn), lambda i, j: (i, j)),
        out_shape=jax.ShapeDtypeStruct((m, n), F32),
        compiler_params=_params("arbitrary", "arbitrary"),
        name="resid_matmul",
    )(a, w, x, mod.table)


def _rotate_half_pairs(y, cos, sin_signed):
    parts = []
    for s in range(y.shape[1] // HEAD_DIM):
        blk = y[:, s * HEAD_DIM:(s + 1) * HEAD_DIM]
        parts.append(blk * cos + pltpu.roll(blk, HEAD_DIM // 2, 1) * sin_signed)
    return jnp.concatenate(parts, axis=1)


def _proj_kernel(h_ref, w_ref, *rest, act, n_scaled, scale):
    o_ref = rest[-1]
    w = w_ref[...].astype(BF16)
    for rows in _row_chunks(h_ref.shape[0], EPILOGUE_CHUNK_ROWS):
        y = _bdot(h_ref[rows, :], w)
        if act == "sigmoid":
            y = jax.nn.sigmoid(y)
        elif act == "rotary":
            y = _rotate_half_pairs(y, rest[0][rows, :], rest[1][rows, :])
        elif act == "scale_leading":
            y = y * jnp.where(pl.program_id(1) < n_scaled, scale, 1.0)
        o_ref[rows, :] = y.astype(o_ref.dtype)


def _proj(h, w, layer, col0, ncols, act, seq, tm, tn, rope=None, n_scaled=0, scale=1.0):
    m, d = h.shape
    j0 = col0 // tn
    in_specs = [_panel_spec(tm, d),
                pl.BlockSpec((None, d, tn), lambda i, j: (layer, 0, j + j0))]
    args = [h, w]
    if act == "rotary":
        spt = seq // tm
        in_specs += [pl.BlockSpec((tm, HEAD_DIM), lambda i, j: (i % spt, 0))] * 2
        args += list(rope)
    return pl.pallas_call(
        functools.partial(_proj_kernel, act=act, n_scaled=n_scaled, scale=scale),
        grid=(m // tm, ncols // tn),
        in_specs=in_specs,
        out_specs=pl.BlockSpec((tm, tn), lambda i, j: (i, j)),
        out_shape=jax.ShapeDtypeStruct((m, ncols), BF16),
        compiler_params=_params("arbitrary", "arbitrary"),
        name="proj_" + act,
    )(*args)


def _branch_kernel(a_ref, b_ref, wa_ref, wb_ref, ga_ref, gb_ref, o_ref):
    wa = wa_ref[...].astype(BF16)
    wb = wb_ref[...].astype(BF16)
    for rows in _row_chunks(a_ref.shape[0], EPILOGUE_CHUNK_ROWS):
        oa = _bdot(a_ref[rows, :], wa)
        ob = _bdot(b_ref[rows, :], wb)
        o_ref[rows, :] = (ga_ref[rows, :].astype(F32) * oa
                          + gb_ref[rows, :].astype(F32) * ob).astype(o_ref.dtype)


def _branch_merge(attn, retg, w_a, w_b, layer, gates, tm, tn):
    m, ka = attn.shape
    kb = retg.shape[1]
    n = w_a.shape[2]
    tn = min(tn, n)
    nb = n // tn
    return pl.pallas_call(
        _branch_kernel,
        grid=(m // tm, nb),
        in_specs=[
            _panel_spec(tm, ka),
            _panel_spec(tm, kb),
            pl.BlockSpec((None, ka, tn), lambda i, j: (layer, 0, j)),
            pl.BlockSpec((None, kb, tn), lambda i, j: (layer, 0, j)),
            pl.BlockSpec((tm, tn), lambda i, j: (i, j)),
            pl.BlockSpec((tm, tn), lambda i, j: (i, j + nb)),
        ],
        out_specs=pl.BlockSpec((tm, tn), lambda i, j: (i, j)),
        out_shape=jax.ShapeDtypeStruct((m, n), BF16),
        compiler_params=_params("arbitrary", "arbitrary"),
        name="branch_merge",
    )(attn, retg, w_a, w_b, gates, gates)


def _sb_kernel(q_ref, k_ref, v_ref, o_ref, rhs_ref, mask_ref, *, tq):
    qi = pl.program_id(2)

    @pl.when((pl.program_id(0) == 0) & (pl.program_id(1) == 0) & (qi == 0))
    def _():
        row = lax.broadcasted_iota(jnp.int32, (tq, tq), 0)
        col = lax.broadcasted_iota(jnp.int32, (tq, tq), 1)
        suffix = jnp.concatenate([(row > col).astype(BF16), jnp.ones((tq, LANES), BF16)], axis=1)
        rhs_ref[...] = jnp.concatenate([suffix, suffix], axis=0)
        mask_ref[...] = (col < row).astype(F32)

    q = q_ref[...]

    def tile(kb, diagonal):
        start = pl.multiple_of(kb * tq, tq)
        k = k_ref[pl.ds(start, tq), :]
        v = v_ref[pl.ds(start, tq), :]
        z = lax.dot_general(q, k, (((1,), (1,)), ((), ())), preferred_element_type=F32)
        log_beta = jnp.minimum(z, 0.0) - jnp.log2(1.0 + jnp.exp2(-jnp.abs(z)))
        log_keep = log_beta - z
        if diagonal:
            log_keep = log_keep * mask_ref[...]
        hi = log_keep.astype(BF16)
        lo = (log_keep - hi.astype(F32)).astype(BF16)
        sums = _bdot(jnp.concatenate([hi, lo], axis=1), rhs_ref[...])
        w = jnp.exp2(log_beta + sums[:, :tq])
        if diagonal:
            w = w * mask_ref[...]
        return _bdot(w.astype(BF16), v), sums[:, tq:]

    def two_tiles(kb, diagonal, carry, scale, acc):
        pv_a, total_a = tile(kb, diagonal)
        pv_b, total_b = tile(jnp.maximum(kb - 1, 0), False)
        has_b = jnp.where(kb > 0, 1.0, 0.0)
        carry_a = carry + total_a
        carry = carry_a + has_b * total_b
        scale_b = jnp.exp2(carry)
        acc = acc + scale * pv_a + (jnp.exp2(carry_a) * has_b) * pv_b
        return jnp.max(scale_b), carry, scale_b, acc

    zeros = jnp.zeros((tq, HEAD_DIM), F32)
    state = two_tiles(qi, True, zeros, jnp.ones((tq, HEAD_DIM), F32), zeros)

    def cond(state):
        return jnp.logical_and(state[0] >= 0, state[1] > 0.0)

    def body(state):
        kb, _, carry, scale, acc = state
        return (kb - 2,) + two_tiles(kb, False, carry, scale, acc)

    acc = lax.while_loop(cond, body, (qi - 2,) + state)[-1]
    o_ref[...] = acc.astype(o_ref.dtype)


def _stick_breaking(qkv, batch, seq, tq):
    m = qkv.shape[0]
    nq = seq // tq
    return pl.pallas_call(
        functools.partial(_sb_kernel, tq=tq),
        grid=(batch, N_HEADS, nq),
        in_specs=[
            pl.BlockSpec((tq, HEAD_DIM), lambda b, h, i: (b * nq + i, h)),
            pl.BlockSpec((seq, HEAD_DIM), lambda b, h, i: (b, N_HEADS + h)),
            pl.BlockSpec((seq, HEAD_DIM), lambda b, h, i: (b, 2 * N_HEADS + h)),
        ],
        out_specs=pl.BlockSpec((tq, HEAD_DIM), lambda b, h, i: (b * nq + i, h)),
        out_shape=jax.ShapeDtypeStruct((m, N_HEADS * HEAD_DIM), BF16),
        scratch_shapes=[pltpu.VMEM((2 * tq, tq + LANES), BF16), pltpu.VMEM((tq, tq), F32)],
        compiler_params=_params("arbitrary", "arbitrary", "arbitrary"),
        name="stick_breaking",
    )(qkv, qkv, qkv)


def _ret_kernel(lg_ref, q_ref, k_ref, v_ref, gate_ref, gn_ref, o_ref, state_ref, entering_ref,
                *, chunk, n_chunks, scale):
    @pl.when(pl.program_id(2) == 0)
    def _():
        state_ref[...] = jnp.zeros_like(state_ref)

    lg = lg_ref[pl.program_id(1)]
    c = chunk
    row = lax.broadcasted_iota(jnp.int32, (c, c), 0).astype(F32)
    col = lax.broadcasted_iota(jnp.int32, (c, c), 1).astype(F32)
    diff = row - col
    decay = jnp.where(diff >= 0, jnp.exp(lg * jnp.maximum(diff, 0.0)), 0.0) * scale
    pos = lax.broadcasted_iota(jnp.int32, (c, HEAD_DIM), 0).astype(F32)
    q_decay = jnp.exp(lg * (pos + 1.0))
    k_decay = jnp.exp(lg * (c - 1.0 - pos)) * scale
    chunk_decay = jnp.exp(jnp.zeros((HEAD_DIM, HEAD_DIM), F32) + lg * c)
    gn = gn_ref[...]

    state = state_ref[...]
    for n in range(n_chunks):
        rows = slice(n * c, (n + 1) * c)
        entering_ref[n] = state.astype(BF16)
        kv = lax.dot_general((k_ref[rows, :].astype(F32) * k_decay).astype(BF16), v_ref[rows, :],
                             (((0,), (0,)), ((), ())), preferred_element_type=F32)
        state = chunk_decay * state + kv
    state_ref[...] = state

    for n in range(n_chunks):
        rows = slice(n * c, (n + 1) * c)
        q = q_ref[rows, :]
        scores = lax.dot_general(q, k_ref[rows, :], (((1,), (1,)), ((), ())),
                                 preferred_element_type=F32) * decay
        out = _bdot(scores.astype(BF16), v_ref[rows, :]) + q_decay * _bdot(q, entering_ref[n])
        mu = jnp.mean(out, axis=-1, keepdims=True)
        cen = out - mu
        var = jnp.mean(cen * cen, axis=-1, keepdims=True)
        normed = cen * lax.rsqrt(var + EPS) * gn
        g = gate_ref[rows, :].astype(F32)
        o_ref[rows, :] = (g * jax.nn.sigmoid(g) * normed).astype(o_ref.dtype)


def _retention(qk, vg, ret_gn, layer, batch, seq, rows_per_step):
    m = qk.shape[0]
    ns = seq // rows_per_step
    log_gamma = jnp.log1p(-jnp.exp2(-5.0 - jnp.arange(N_HEADS, dtype=F32)))
    blk = (rows_per_step, HEAD_DIM)
    return pl.pallas_call(
        functools.partial(_ret_kernel, chunk=RET_CHUNK, n_chunks=rows_per_step // RET_CHUNK,
                          scale=HEAD_DIM ** -0.5),
        grid=(batch, N_HEADS, ns),
        in_specs=[
            pl.BlockSpec(memory_space=pltpu.SMEM),
            pl.BlockSpec(blk, lambda b, h, s: (b * ns + s, h)),
            pl.BlockSpec(blk, lambda b, h, s: (b * ns + s, N_HEADS + h)),
            pl.BlockSpec(blk, lambda b, h, s: (b * ns + s, h)),
            pl.BlockSpec(blk, lambda b, h, s: (b * ns + s, N_HEADS + h)),
            pl.BlockSpec((None, 1, HEAD_DIM), lambda b, h, s: (layer * N_HEADS + h, 0, 0)),
        ],
        out_specs=pl.BlockSpec(blk, lambda b, h, s: (b * ns + s, h)),
        out_shape=jax.ShapeDtypeStruct((m, N_HEADS * HEAD_DIM), BF16),
        scratch_shapes=[pltpu.VMEM((HEAD_DIM, HEAD_DIM), F32),
                        pltpu.VMEM((rows_per_step // RET_CHUNK, HEAD_DIM, HEAD_DIM), BF16)],
        compiler_params=_params("arbitrary", "arbitrary", "arbitrary"),
        name="retention",
    )(log_gamma, qk, qk, vg, vg, ret_gn.reshape(-1, 1, HEAD_DIM))


def _rope_tables(seq):
    half = HEAD_DIM // 2
    inv_freq = ROPE_BASE ** (-jnp.arange(half, dtype=F32) / half)
    ang = jnp.arange(seq, dtype=jnp.int32).astype(F32)[:, None] * inv_freq[None, :]
    cos, sin = jnp.cos(ang), jnp.sin(ang)
    return (jnp.concatenate([cos, cos], axis=-1), jnp.concatenate([-sin, sin], axis=-1))


def kernel(x, c, ada_down, ada_up, ada_bias, norm_ffn1, ffn1_in, ffn1_out, norm_mix, w_in, ret_gn,
           w_branch_a, w_branch_b, w_out, norm_ffn2, ffn2_in, ffn2_out, norm_final):
    batch, seq, d = x.shape
    depth = ada_down.shape[0]
    width = N_HEADS * HEAD_DIM
    tm_wide = min(2048, seq)
    tm_deep = min(1024, seq)
    tq = min(256, seq)
    tn = 512
    rope = _rope_tables(seq)
    table = _ada_modulation(c, ada_down, ada_up, ada_bias)
    xf = x.reshape(batch * seq, d)
    for l in range(depth):
        mod = _Mod(table, l, batch, seq)
        h = _rms_norm(xf, norm_ffn1, l, seq, mod, 0, 1)
        xf = _resid_matmul(_ffn_in(h, ffn1_in, l, tm_wide), ffn1_out, l, xf, mod, 2, 0.5,
                           tm_deep, MXU_DIM)
        h = _rms_norm(xf, norm_mix, l, seq, mod, 3, 4)
        qkv = _proj(h, w_in, l, 0, 3 * width, "scale_leading", seq, tm_wide, tn,
                    n_scaled=width // tn, scale=HEAD_DIM ** -0.5 * LOG2_E)
        qk = _proj(h, w_in, l, 3 * width, 2 * width, "rotary", seq, tm_wide, tn, rope)
        vg = _proj(h, w_in, l, 5 * width, 2 * width, "none", seq, tm_wide, tn)
        gates = _proj(h, w_in, l, 7 * width, 2 * d, "sigmoid", seq, tm_wide, tn)
        attn = _stick_breaking(qkv, batch, seq, tq)
        retg = _retention(qk, vg, ret_gn, l, batch, seq, min(1024, seq))
        merged = _branch_merge(attn, retg, w_branch_a, w_branch_b, l, gates, tm_wide, tn)
        xf = _resid_matmul(merged, w_out, l, xf, mod, 5, 1.0, tm_wide, tn)
        h = _rms_norm(xf, norm_ffn2, l, seq, mod, 6, 7)
        xf = _resid_matmul(_ffn_in(h, ffn2_in, l, tm_wide), ffn2_out, l, xf, mod, 8, 0.5,
                           tm_deep, MXU_DIM)
    out = _rms_norm(xf, norm_final.reshape(1, d), 0, seq, out_dtype=x.dtype)
    return out.reshape(batch, seq, d)
```

```python
import functools
import math

import jax
import jax.numpy as jnp
from jax import lax
from jax.experimental import pallas as pl
from jax.experimental.pallas import tpu as pltpu

BF16 = jnp.bfloat16
F32 = jnp.float32

LANES = 128
MXU_DIM = 256
VMEM_LIMIT_BYTES = 56 * 1024 * 1024

HEAD_DIM = 128
N_HEADS = 16
N_MOD = 9
RET_CHUNK = 128
ROPE_BASE = 10000.0
EPS = 1e-6
LOG2_E = math.log2(math.e)


def _params(*semantics):
    return pltpu.CompilerParams(dimension_semantics=semantics,
                                vmem_limit_bytes=VMEM_LIMIT_BYTES)


def _bdot(a, b):
    return jnp.dot(a, b, preferred_element_type=F32)


def _panel_spec(tm, k):
    return pl.BlockSpec((tm, k), lambda i, j: (i, 0), pipeline_mode=pl.Buffered(1))


def _ada_kernel(c_ref, down_ref, up_ref, bias_ref, o_ref, t_ref):
    @pl.when(pl.program_id(1) == 0)
    def _():
        c = c_ref[...]
        t_ref[...] = jnp.dot(c * jax.nn.sigmoid(c), down_ref[...], preferred_element_type=F32,
                             precision=lax.Precision.HIGHEST)

    o_ref[...] = jnp.dot(t_ref[...], up_ref[...], preferred_element_type=F32,
                         precision=lax.Precision.HIGHEST) + bias_ref[...]


def _ada_modulation(c, ada_down, ada_up, ada_bias):
    depth, d, r = ada_down.shape
    n = ada_up.shape[2]
    b = c.shape[0]
    rows = 8
    c_pad = jnp.zeros((rows, d), F32).at[:b].set(c)
    tn = min(n, 4096)
    out = pl.pallas_call(
        _ada_kernel,
        grid=(depth, n // tn),
        in_specs=[
            pl.BlockSpec((rows, d), lambda l, j: (0, 0)),
            pl.BlockSpec((None, d, r), lambda l, j: (l, 0, 0)),
            pl.BlockSpec((None, r, tn), lambda l, j: (l, 0, j)),
            pl.BlockSpec((None, 1, tn), lambda l, j: (l, 0, j)),
        ],
        out_specs=pl.BlockSpec((None, rows, tn), lambda l, j: (l, 0, j)),
        out_shape=jax.ShapeDtypeStruct((depth, rows, n), F32),
        scratch_shapes=[pltpu.VMEM((rows, r), F32)],
        compiler_params=_params("arbitrary", "arbitrary"),
        name="ada_modulation",
    )(c_pad, ada_down, ada_up, ada_bias.reshape(depth, 1, n))
    return out[:, :b].reshape(depth * b * N_MOD, 1, d)


class _Mod:
    def __init__(self, table, layer, batch, seq):
        self.table, self.base, self.seq = table, layer * batch * N_MOD, seq

    def row(self, first_token, idx):
        return self.base + (first_token // self.seq) * N_MOD + idx


def _norm_kernel(x_ref, g_ref, *rest, modulated):
    o_ref = rest[-1]
    x = x_ref[...]
    y = x * lax.rsqrt(jnp.mean(x * x, axis=-1, keepdims=True) + EPS) * g_ref[...]
    if modulated:
        sh_ref, sc_ref = rest[0], rest[1]
        y = y * (1.0 + sc_ref[...]) + sh_ref[...]
    o_ref[...] = y.astype(o_ref.dtype)


def _rms_norm(x, g, layer, seq, mod=None, shift_idx=0, scale_idx=0, out_dtype=None):
    m, d = x.shape
    tm = min(512, seq)
    in_specs = [pl.BlockSpec((tm, d), lambda i: (i, 0)),
                pl.BlockSpec((None, 1, d), lambda i: (layer, 0, 0))]
    args = [x, g.reshape(-1, 1, d)]
    if mod is not None:
        def mod_spec(idx):
            return pl.BlockSpec((None, 1, d), lambda i: (mod.row(i * tm, idx), 0, 0))
        in_specs += [mod_spec(shift_idx), mod_spec(scale_idx)]
        args += [mod.table, mod.table]
    return pl.pallas_call(
        functools.partial(_norm_kernel, modulated=mod is not None),
        grid=(m // tm,),
        in_specs=in_specs,
        out_specs=pl.BlockSpec((tm, d), lambda i: (i, 0)),
        out_shape=jax.ShapeDtypeStruct((m, d), out_dtype or BF16),
        compiler_params=_params("arbitrary"),
        name="rms_norm",
    )(*args)


EPILOGUE_CHUNK_ROWS = 512


def _row_chunks(tm, rows):
    rows = min(rows, tm)
    return [slice(r, r + rows) for r in range(0, tm, rows)]


def _ffn_in_kernel(h_ref, wa_ref, wb_ref, o_ref):
    wa = wa_ref[...].astype(BF16)
    wb = wb_ref[...].astype(BF16)
    for rows in _row_chunks(h_ref.shape[0], EPILOGUE_CHUNK_ROWS):
        h = h_ref[rows, :]
        a = _bdot(h, wa)
        b = _bdot(h, wb)
        o_ref[rows, :] = (a * jax.nn.sigmoid(a) * b).astype(o_ref.dtype)


def _ffn_in(h, w_in, layer, tm):
    m, d = h.shape
    f = w_in.shape[2] // 2
    tn = MXU_DIM
    nb = f // tn
    return pl.pallas_call(
        _ffn_in_kernel,
        grid=(m // tm, nb),
        in_specs=[
            _panel_spec(tm, d),
            pl.BlockSpec((None, d, tn), lambda i, j: (layer, 0, j)),
            pl.BlockSpec((None, d, tn), lambda i, j: (layer, 0, j + nb)),
        ],
        out_specs=pl.BlockSpec((tm, tn), lambda i, j: (i, j)),
        out_shape=jax.ShapeDtypeStruct((m, f), BF16),
        compiler_params=_params("arbitrary", "arbitrary"),
        name="ffn_in",
    )(h, w_in, w_in)


def _resid_kernel(a_ref, w_ref, x_ref, g_ref, o_ref, *, coef):
    w = w_ref[...].astype(BF16)
    gate = coef * g_ref[...]
    for rows in _row_chunks(a_ref.shape[0], EPILOGUE_CHUNK_ROWS):
        o_ref[rows, :] = x_ref[rows, :] + gate * _bdot(a_ref[rows, :], w)


def _resid_matmul(a, w, layer, x, mod, gate_idx, coef, tm, tn):
    m, k = a.shape
    n = w.shape[2]
    tn = min(tn, n)
    return pl.pallas_call(
        functools.partial(_resid_kernel, coef=coef),
        grid=(m // tm, n // tn),
        in_specs=[
            _panel_spec(tm, k),
            pl.BlockSpec((None, k, tn), lambda i, j: (layer, 0, j)),
            pl.BlockSpec((tm, tn), lambda i, j: (i, j)),
            pl.BlockSpec((None, 1, tn), lambda i, j: (mod.row(i * tm, gate_idx), 0, j)),
        ],
        out_specs=pl.BlockSpec((tm, tn), lambda i, j: (i, j)),
        out_shape=jax.ShapeDtypeStruct((m, n), F32),
        compiler_params=_params("arbitrary", "arbitrary"),
        name="resid_matmul",
    )(a, w, x, mod.table)


def _rotate_half_pairs(y, cos, sin_signed):
    parts = []
    for s in range(y.shape[1] // HEAD_DIM):
        blk = y[:, s * HEAD_DIM:(s + 1) * HEAD_DIM]
        parts.append(blk * cos + pltpu.roll(blk, HEAD_DIM // 2, 1) * sin_signed)
    return jnp.concatenate(parts, axis=1)


def _in_proj_kernel(h_ref, w_ref, cos_ref, sin_ref, o_ref, *, q_end, rot_start, rot_end,
                    gate_start, q_scale):
    j = pl.program_id(1)

    def run(act):
        w = w_ref[...].astype(BF16)
        for rows in _row_chunks(h_ref.shape[0], EPILOGUE_CHUNK_ROWS):
            y = _bdot(h_ref[rows, :], w)
            if act == "scale":
                y = y * q_scale
            elif act == "rotary":
                y = _rotate_half_pairs(y, cos_ref[rows, :], sin_ref[rows, :])
            elif act == "sigmoid":
                y = jax.nn.sigmoid(y)
            o_ref[rows, :] = y.astype(o_ref.dtype)

    rotary = (j >= rot_start) & (j < rot_end)
    pl.when(j < q_end)(lambda: run("scale"))
    pl.when(rotary)(lambda: run("rotary"))
    pl.when(j >= gate_start)(lambda: run("sigmoid"))
    pl.when((j >= q_end) & (j < gate_start) & jnp.logical_not(rotary))(lambda: run("none"))


def _in_proj(h, w, layer, seq, tm, tn, rope, width, q_scale):
    m, d = h.shape
    ncols = w.shape[2]
    spt = seq // tm
    return pl.pallas_call(
        functools.partial(_in_proj_kernel, q_end=width // tn, rot_start=3 * width // tn,
                          rot_end=5 * width // tn, gate_start=7 * width // tn, q_scale=q_scale),
        grid=(m // tm, ncols // tn),
        in_specs=[
            _panel_spec(tm, d),
            pl.BlockSpec((None, d, tn), lambda i, j: (layer, 0, j)),
            pl.BlockSpec((tm, HEAD_DIM), lambda i, j: (i % spt, 0)),
            pl.BlockSpec((tm, HEAD_DIM), lambda i, j: (i % spt, 0)),
        ],
        out_specs=pl.BlockSpec((tm, tn), lambda i, j: (i, j)),
        out_shape=jax.ShapeDtypeStruct((m, ncols), BF16),
        compiler_params=_params("arbitrary", "arbitrary"),
        name="in_proj",
    )(h, w, *rope)


def _branch_kernel(a_ref, b_ref, wa_ref, wb_ref, ga_ref, gb_ref, o_ref):
    wa = wa_ref[...].astype(BF16)
    wb = wb_ref[...].astype(BF16)
    for rows in _row_chunks(a_ref.shape[0], EPILOGUE_CHUNK_ROWS):
        oa = _bdot(a_ref[rows, :], wa)
        ob = _bdot(b_ref[rows, :], wb)
        o_ref[rows, :] = (ga_ref[rows, :].astype(F32) * oa
                          + gb_ref[rows, :].astype(F32) * ob).astype(o_ref.dtype)


def _branch_merge(attn, retg, w_a, w_b, layer, proj, gate_col0, tm, tn):
    m, ka = attn.shape
    kb = retg.shape[1]
    n = w_a.shape[2]
    tn = min(tn, n)
    nb = n // tn
    ja, jb = gate_col0 // tn, (gate_col0 + n) // tn
    return pl.pallas_call(
        _branch_kernel,
        grid=(m // tm, nb),
        in_specs=[
            _panel_spec(tm, ka),
            _panel_spec(tm, kb),
            pl.BlockSpec((None, ka, tn), lambda i, j: (layer, 0, j)),
            pl.BlockSpec((None, kb, tn), lambda i, j: (layer, 0, j)),
            pl.BlockSpec((tm, tn), lambda i, j: (i, j + ja)),
            pl.BlockSpec((tm, tn), lambda i, j: (i, j + jb)),
        ],
        out_specs=pl.BlockSpec((tm, tn), lambda i, j: (i, j)),
        out_shape=jax.ShapeDtypeStruct((m, n), BF16),
        compiler_params=_params("arbitrary", "arbitrary"),
        name="branch_merge",
    )(attn, retg, w_a, w_b, proj, proj)


PAIR = 2


def _sb_kernel(q_ref, k_ref, v_ref, o_ref, rhs_ref, mask_ref, *, ts, n_sub, n_first):
    @pl.when((pl.program_id(0) == 0) & (pl.program_id(1) == 0) & (pl.program_id(2) == 0))
    def _():
        row = lax.broadcasted_iota(jnp.int32, (ts, ts), 0)
        col = lax.broadcasted_iota(jnp.int32, (ts, ts), 1)
        suffix = jnp.concatenate([(row > col).astype(BF16), jnp.ones((ts, LANES), BF16)], axis=1)
        rhs_ref[...] = jnp.concatenate([suffix, suffix], axis=0)
        mask_ref[...] = (col < row).astype(F32)

    def weights_times_v(specs):
        starts = [pl.multiple_of(kb * ts, ts) for _, kb, _ in specs]
        zs = [lax.dot_general(q, k_ref[pl.ds(start, ts), :], (((1,), (1,)), ((), ())),
                              preferred_element_type=F32)
              for (q, _, _), start in zip(specs, starts)]
        log_betas, splits = [], []
        for (_, _, diagonal), z in zip(specs, zs):
            log_beta = jnp.minimum(z, 0.0) - jnp.log2(1.0 + jnp.exp2(-jnp.abs(z)))
            log_keep = log_beta - z
            if diagonal:
                log_keep = log_keep * mask_ref[...]
            hi = log_keep.astype(BF16)
            lo = (log_keep - hi.astype(F32)).astype(BF16)
            log_betas.append(log_beta)
            splits.append(jnp.concatenate([hi, lo], axis=1))
        sums = [_bdot(split, rhs_ref[...]) for split in splits]
        out = []
        for (_, _, diagonal), start, log_beta, s in zip(specs, starts, log_betas, sums):
            w = jnp.exp2(log_beta + s[:, :ts])
            if diagonal:
                w = w * mask_ref[...]
            out.append((_bdot(w.astype(BF16), v_ref[pl.ds(start, ts), :]), s[:, ts:]))
        return out

    def tile_specs(q, kb, count, diagonal):
        return [(q, jnp.maximum(kb - t, 0), diagonal and t == 0) for t in range(count)]

    def fold(results, kb, diagonal, carry, scale, acc):
        for t, (pv, total) in enumerate(results):
            if t > 0 or not diagonal:
                valid = jnp.where(kb - t >= 0, 1.0, 0.0)
                scale, total = scale * valid, total * valid
            acc = acc + scale * pv
            carry = carry + total
            scale = jnp.exp2(carry)
        return carry, scale, acc

    qs = [pl.program_id(2) * n_sub + sub for sub in range(n_sub)]
    q = [q_ref[sub * ts:(sub + 1) * ts, :] for sub in range(n_sub)]

    def advance(kbs, count, diagonal, states):
        results = weights_times_v([spec for sub in range(n_sub)
                                   for spec in tile_specs(q[sub], kbs[sub], count, diagonal)])
        live, new_states = 0.0, []
        for sub in range(n_sub):
            state = fold(results[sub * count:(sub + 1) * count], kbs[sub], diagonal, *states[sub])
            live = jnp.maximum(live, jnp.where(kbs[sub] - count >= 0, jnp.max(state[1]), 0.0))
            new_states.append(state)
        return live, tuple(new_states)

    zeros = jnp.zeros((ts, HEAD_DIM), F32)
    live, states = advance(qs, n_first, True, [(zeros, jnp.ones((ts, HEAD_DIM), F32), zeros)] * n_sub)

    def body(loop_state):
        it, _, states = loop_state
        kbs = [qs[sub] - n_first - PAIR * it for sub in range(n_sub)]
        return (it + 1,) + advance(kbs, PAIR, False, states)

    states = lax.while_loop(lambda loop_state: loop_state[1] > 0.0, body, (0, live, states))[2]
    for sub in range(n_sub):
        o_ref[sub * ts:(sub + 1) * ts, :] = states[sub][2].astype(o_ref.dtype)


def _stick_breaking(qkv, batch, seq, tq, ts):
    m = qkv.shape[0]
    nq = seq // tq
    return pl.pallas_call(
        functools.partial(_sb_kernel, ts=ts, n_sub=tq // ts, n_first=3),
        grid=(batch, N_HEADS, nq),
        in_specs=[
            pl.BlockSpec((tq, HEAD_DIM), lambda b, h, i: (b * nq + i, h)),
            pl.BlockSpec((seq, HEAD_DIM), lambda b, h, i: (b, N_HEADS + h)),
            pl.BlockSpec((seq, HEAD_DIM), lambda b, h, i: (b, 2 * N_HEADS + h)),
        ],
        out_specs=pl.BlockSpec((tq, HEAD_DIM), lambda b, h, i: (b * nq + i, h)),
        out_shape=jax.ShapeDtypeStruct((m, N_HEADS * HEAD_DIM), BF16),
        scratch_shapes=[pltpu.VMEM((2 * ts, ts + LANES), BF16), pltpu.VMEM((ts, ts), F32)],
        compiler_params=_params("arbitrary", "arbitrary", "arbitrary"),
        name="stick_breaking",
    )(qkv, qkv, qkv)


def _ret_kernel(lg_ref, q_ref, k_ref, v_ref, gate_ref, gn_ref, o_ref, state_ref, entering_ref,
                *, chunk, n_chunks, scale):
    @pl.when(pl.program_id(2) == 0)
    def _():
        state_ref[...] = jnp.zeros_like(state_ref)

    lg = lg_ref[pl.program_id(1)]
    c = chunk
    row = lax.broadcasted_iota(jnp.int32, (c, c), 0).astype(F32)
    col = lax.broadcasted_iota(jnp.int32, (c, c), 1).astype(F32)
    diff = row - col
    decay = jnp.where(diff >= 0, jnp.exp(lg * jnp.maximum(diff, 0.0)), 0.0) * scale
    pos = lax.broadcasted_iota(jnp.int32, (c, HEAD_DIM), 0).astype(F32)
    q_decay = jnp.exp(lg * (pos + 1.0))
    k_decay = jnp.exp(lg * (c - 1.0 - pos)) * scale
    chunk_decay = jnp.exp(jnp.zeros((HEAD_DIM, HEAD_DIM), F32) + lg * c)
    gn = gn_ref[...]

    state = state_ref[...]
    for n in range(n_chunks):
        rows = slice(n * c, (n + 1) * c)
        entering_ref[n] = state.astype(BF16)
        kv = lax.dot_general((k_ref[rows, :].astype(F32) * k_decay).astype(BF16), v_ref[rows, :],
                             (((0,), (0,)), ((), ())), preferred_element_type=F32)
        state = chunk_decay * state + kv
    state_ref[...] = state

    for n in range(n_chunks):
        rows = slice(n * c, (n + 1) * c)
        q = q_ref[rows, :]
        scores = lax.dot_general(q, k_ref[rows, :], (((1,), (1,)), ((), ())),
                                 preferred_element_type=F32) * decay
        out = _bdot(scores.astype(BF16), v_ref[rows, :]) + q_decay * _bdot(q, entering_ref[n])
        mu = jnp.mean(out, axis=-1, keepdims=True)
        cen = out - mu
        var = jnp.mean(cen * cen, axis=-1, keepdims=True)
        normed = cen * lax.rsqrt(var + EPS) * gn
        g = gate_ref[rows, :].astype(F32)
        o_ref[rows, :] = (g * jax.nn.sigmoid(g) * normed).astype(o_ref.dtype)


def _retention(proj, head0, ret_gn, layer, batch, seq, rows_per_step):
    m = proj.shape[0]
    ns = seq // rows_per_step
    log_gamma = jnp.log1p(-jnp.exp2(-5.0 - jnp.arange(N_HEADS, dtype=F32)))
    blk = (rows_per_step, HEAD_DIM)

    def part(p):
        return pl.BlockSpec(blk, lambda b, h, s: (b * ns + s, head0 + p * N_HEADS + h))

    return pl.pallas_call(
        functools.partial(_ret_kernel, chunk=RET_CHUNK, n_chunks=rows_per_step // RET_CHUNK,
                          scale=HEAD_DIM ** -0.5),
        grid=(batch, N_HEADS, ns),
        in_specs=[
            pl.BlockSpec(memory_space=pltpu.SMEM),
            part(0), part(1), part(2), part(3),
            pl.BlockSpec((None, 1, HEAD_DIM), lambda b, h, s: (layer * N_HEADS + h, 0, 0)),
        ],
        out_specs=pl.BlockSpec(blk, lambda b, h, s: (b * ns + s, h)),
        out_shape=jax.ShapeDtypeStruct((m, N_HEADS * HEAD_DIM), BF16),
        scratch_shapes=[pltpu.VMEM((HEAD_DIM, HEAD_DIM), F32),
                        pltpu.VMEM((rows_per_step // RET_CHUNK, HEAD_DIM, HEAD_DIM), BF16)],
        compiler_params=_params("arbitrary", "arbitrary", "arbitrary"),
        name="retention",
    )(log_gamma, proj, proj, proj, proj, ret_gn.reshape(-1, 1, HEAD_DIM))


def _rope_tables(seq):
    half = HEAD_DIM // 2
    inv_freq = ROPE_BASE ** (-jnp.arange(half, dtype=F32) / half)
    ang = jnp.arange(seq, dtype=jnp.int32).astype(F32)[:, None] * inv_freq[None, :]
    cos, sin = jnp.cos(ang), jnp.sin(ang)
    return (jnp.concatenate([cos, cos], axis=-1), jnp.concatenate([-sin, sin], axis=-1))


def kernel(x, c, ada_down, ada_up, ada_bias, norm_ffn1, ffn1_in, ffn1_out, norm_mix, w_in, ret_gn,
           w_branch_a, w_branch_b, w_out, norm_ffn2, ffn2_in, ffn2_out, norm_final):
    batch, seq, d = x.shape
    depth = ada_down.shape[0]
    width = N_HEADS * HEAD_DIM
    tm_wide = min(2048, seq)
    tm_deep = min(1024, seq)
    tq = min(512, seq)
    tn = 512
    rope = _rope_tables(seq)
    table = _ada_modulation(c, ada_down, ada_up, ada_bias)
    xf = x.reshape(batch * seq, d)
    for l in range(depth):
        mod = _Mod(table, l, batch, seq)
        h = _rms_norm(xf, norm_ffn1, l, seq, mod, 0, 1)
        xf = _resid_matmul(_ffn_in(h, ffn1_in, l, tm_wide), ffn1_out, l, xf, mod, 2, 0.5,
                           tm_deep, MXU_DIM)
        h = _rms_norm(xf, norm_mix, l, seq, mod, 3, 4)
        proj = _in_proj(h, w_in, l, seq, tm_wide, tn, rope, width, HEAD_DIM ** -0.5 * LOG2_E)
        attn = _stick_breaking(proj, batch, seq, tq, HEAD_DIM)
        retg = _retention(proj, 3 * N_HEADS, ret_gn, l, batch, seq, min(2048, seq))
        merged = _branch_merge(attn, retg, w_branch_a, w_branch_b, l, proj, 7 * width, tm_wide, tn)
        xf = _resid_matmul(merged, w_out, l, xf, mod, 5, 1.0, tm_wide, tn)
        h = _rms_norm(xf, norm_ffn2, l, seq, mod, 6, 7)
        xf = _resid_matmul(_ffn_in(h, ffn2_in, l, tm_wide), ffn2_out, l, xf, mod, 8, 0.5,
                           tm_deep, MXU_DIM)
    out = _rms_norm(xf, norm_final.reshape(1, d), 0, seq, out_dtype=x.dtype)
    return out.reshape(batch, seq, d)
```

```python
import functools
import math

import jax
import jax.numpy as jnp
from jax import lax
from jax.experimental import pallas as pl
from jax.experimental.pallas import tpu as pltpu

BF16 = jnp.bfloat16
F32 = jnp.float32

LANES = 128
MXU_DIM = 256
VMEM_LIMIT_BYTES = 56 * 1024 * 1024

HEAD_DIM = 128
N_HEADS = 16
N_MOD = 9
RET_CHUNK = 128
ROPE_BASE = 10000.0
EPS = 1e-6
LOG2_E = math.log2(math.e)


def _params(*semantics):
    return pltpu.CompilerParams(dimension_semantics=semantics,
                                vmem_limit_bytes=VMEM_LIMIT_BYTES)


def _bdot(a, b):
    return jnp.dot(a, b, preferred_element_type=F32)


def _panel_spec(tm, k):
    return pl.BlockSpec((tm, k), lambda i, j: (i, 0), pipeline_mode=pl.Buffered(1))


def _ada_kernel(c_ref, down_ref, up_ref, bias_ref, o_ref, t_ref):
    @pl.when(pl.program_id(1) == 0)
    def _():
        c = c_ref[...]
        t_ref[...] = jnp.dot(c * jax.nn.sigmoid(c), down_ref[...], preferred_element_type=F32,
                             precision=lax.Precision.HIGHEST)

    o_ref[...] = jnp.dot(t_ref[...], up_ref[...], preferred_element_type=F32,
                         precision=lax.Precision.HIGHEST) + bias_ref[...]


def _ada_modulation(c, ada_down, ada_up, ada_bias):
    depth, d, r = ada_down.shape
    n = ada_up.shape[2]
    b = c.shape[0]
    rows = 8
    c_pad = jnp.zeros((rows, d), F32).at[:b].set(c)
    tn = min(n, 4096)
    out = pl.pallas_call(
        _ada_kernel,
        grid=(depth, n // tn),
        in_specs=[
            pl.BlockSpec((rows, d), lambda l, j: (0, 0)),
            pl.BlockSpec((None, d, r), lambda l, j: (l, 0, 0)),
            pl.BlockSpec((None, r, tn), lambda l, j: (l, 0, j)),
            pl.BlockSpec((None, 1, tn), lambda l, j: (l, 0, j)),
        ],
        out_specs=pl.BlockSpec((None, rows, tn), lambda l, j: (l, 0, j)),
        out_shape=jax.ShapeDtypeStruct((depth, rows, n), F32),
        scratch_shapes=[pltpu.VMEM((rows, r), F32)],
        compiler_params=_params("arbitrary", "arbitrary"),
        name="ada_modulation",
    )(c_pad, ada_down, ada_up, ada_bias.reshape(depth, 1, n))
    return out[:, :b].reshape(depth * b * N_MOD, 1, d)


class _Mod:
    def __init__(self, table, layer, batch, seq):
        self.table, self.base, self.seq = table, layer * batch * N_MOD, seq

    def row(self, first_token, idx):
        return self.base + (first_token // self.seq) * N_MOD + idx


NORM_ROWS = 16
NORM_UNROLL = 4


def _norm_kernel(x_ref, g_ref, *rest, modulated):
    o_ref, r_ref = rest[-2], rest[-1]
    gain = g_ref[...]
    if modulated:
        sh_ref, sc_ref = rest[0], rest[1]
        gain = gain * (1.0 + sc_ref[...])
    trips = x_ref.shape[0] // NORM_ROWS

    def group(r):
        return pl.ds(pl.multiple_of(r * NORM_ROWS, NORM_ROWS), NORM_ROWS)

    def factors(r, carry):
        x = x_ref[group(r), :]
        r_ref[group(r), :] = jnp.zeros((NORM_ROWS, LANES), F32) + lax.rsqrt(
            jnp.mean(x * x, axis=-1, keepdims=True) + EPS)
        return carry

    def scale(r, carry):
        factor = r_ref[group(r), :]
        y = x_ref[group(r), :] * jnp.concatenate([factor] * (x_ref.shape[1] // LANES), axis=1) * gain
        if modulated:
            y = y + sh_ref[...]
        o_ref[group(r), :] = y.astype(o_ref.dtype)
        return carry

    lax.fori_loop(0, trips, factors, 0, unroll=NORM_UNROLL)
    lax.fori_loop(0, trips, scale, 0, unroll=NORM_UNROLL)


def _rms_norm(x, g, layer, seq, mod=None, shift_idx=0, scale_idx=0, out_dtype=None):
    m, d = x.shape
    tm = min(512, seq)
    in_specs = [pl.BlockSpec((tm, d), lambda i: (i, 0)),
                pl.BlockSpec((None, 1, d), lambda i: (layer, 0, 0))]
    args = [x, g.reshape(-1, 1, d)]
    if mod is not None:
        def mod_spec(idx):
            return pl.BlockSpec((None, 1, d), lambda i: (mod.row(i * tm, idx), 0, 0))
        in_specs += [mod_spec(shift_idx), mod_spec(scale_idx)]
        args += [mod.table, mod.table]
    return pl.pallas_call(
        functools.partial(_norm_kernel, modulated=mod is not None),
        grid=(m // tm,),
        in_specs=in_specs,
        out_specs=pl.BlockSpec((tm, d), lambda i: (i, 0)),
        out_shape=jax.ShapeDtypeStruct((m, d), out_dtype or BF16),
        scratch_shapes=[pltpu.VMEM((tm, LANES), F32)],
        compiler_params=_params("arbitrary"),
        name="rms_norm",
    )(*args)


EPILOGUE_CHUNK_ROWS = 512


def _row_chunks(tm, rows):
    rows = min(rows, tm)
    return [slice(r, r + rows) for r in range(0, tm, rows)]


def _ffn_in_kernel(h_ref, wa_ref, wb_ref, o_ref):
    wa = wa_ref[...].astype(BF16)
    wb = wb_ref[...].astype(BF16)
    for rows in _row_chunks(h_ref.shape[0], EPILOGUE_CHUNK_ROWS):
        h = h_ref[rows, :]
        a = _bdot(h, wa)
        b = _bdot(h, wb)
        o_ref[rows, :] = (a * jax.nn.sigmoid(a) * b).astype(o_ref.dtype)


def _ffn_in(h, w_in, layer, tm):
    m, d = h.shape
    f = w_in.shape[2] // 2
    tn = MXU_DIM
    nb = f // tn
    return pl.pallas_call(
        _ffn_in_kernel,
        grid=(m // tm, nb),
        in_specs=[
            _panel_spec(tm, d),
            pl.BlockSpec((None, d, tn), lambda i, j: (layer, 0, j)),
            pl.BlockSpec((None, d, tn), lambda i, j: (layer, 0, j + nb)),
        ],
        out_specs=pl.BlockSpec((tm, tn), lambda i, j: (i, j)),
        out_shape=jax.ShapeDtypeStruct((m, f), BF16),
        compiler_params=_params("arbitrary", "arbitrary"),
        name="ffn_in",
    )(h, w_in, w_in)


def _resid_kernel(a_ref, w_ref, x_ref, g_ref, o_ref, *, coef):
    w = w_ref[...].astype(BF16)
    gate = coef * g_ref[...]
    for rows in _row_chunks(a_ref.shape[0], EPILOGUE_CHUNK_ROWS):
        o_ref[rows, :] = x_ref[rows, :] + gate * _bdot(a_ref[rows, :], w)


def _resid_matmul(a, w, layer, x, mod, gate_idx, coef, tm, tn):
    m, k = a.shape
    n = w.shape[2]
    tn = min(tn, n)
    return pl.pallas_call(
        functools.partial(_resid_kernel, coef=coef),
        grid=(m // tm, n // tn),
        in_specs=[
            _panel_spec(tm, k),
            pl.BlockSpec((None, k, tn), lambda i, j: (layer, 0, j)),
            pl.BlockSpec((tm, tn), lambda i, j: (i, j)),
            pl.BlockSpec((None, 1, tn), lambda i, j: (mod.row(i * tm, gate_idx), 0, j)),
        ],
        out_specs=pl.BlockSpec((tm, tn), lambda i, j: (i, j)),
        out_shape=jax.ShapeDtypeStruct((m, n), F32),
        compiler_params=_params("arbitrary", "arbitrary"),
        name="resid_matmul",
    )(a, w, x, mod.table)


def _rotate_half_pairs(y, cos, sin_signed):
    parts = []
    for s in range(y.shape[1] // HEAD_DIM):
        blk = y[:, s * HEAD_DIM:(s + 1) * HEAD_DIM]
        parts.append(blk * cos + pltpu.roll(blk, HEAD_DIM // 2, 1) * sin_signed)
    return jnp.concatenate(parts, axis=1)


def _in_proj_kernel(h_ref, w_ref, cos_ref, sin_ref, o_ref, *, q_end, rot_start, rot_end,
                    gate_start, q_scale):
    j = pl.program_id(1)

    def run(act):
        w = w_ref[...].astype(BF16)
        for rows in _row_chunks(h_ref.shape[0], EPILOGUE_CHUNK_ROWS):
            y = _bdot(h_ref[rows, :], w)
            if act == "scale":
                y = y * q_scale
            elif act == "rotary":
                y = _rotate_half_pairs(y, cos_ref[rows, :], sin_ref[rows, :])
            elif act == "sigmoid":
                y = jax.nn.sigmoid(y)
            o_ref[rows, :] = y.astype(o_ref.dtype)

    rotary = (j >= rot_start) & (j < rot_end)
    pl.when(j < q_end)(lambda: run("scale"))
    pl.when(rotary)(lambda: run("rotary"))
    pl.when(j >= gate_start)(lambda: run("sigmoid"))
    pl.when((j >= q_end) & (j < gate_start) & jnp.logical_not(rotary))(lambda: run("none"))


def _in_proj(h, w, layer, seq, tm, tn, rope, width, q_scale):
    m, d = h.shape
    ncols = w.shape[2]
    spt = seq // tm
    return pl.pallas_call(
        functools.partial(_in_proj_kernel, q_end=width // tn, rot_start=3 * width // tn,
                          rot_end=5 * width // tn, gate_start=7 * width // tn, q_scale=q_scale),
        grid=(m // tm, ncols // tn),
        in_specs=[
            _panel_spec(tm, d),
            pl.BlockSpec((None, d, tn), lambda i, j: (layer, 0, j)),
            pl.BlockSpec((tm, HEAD_DIM), lambda i, j: (i % spt, 0)),
            pl.BlockSpec((tm, HEAD_DIM), lambda i, j: (i % spt, 0)),
        ],
        out_specs=pl.BlockSpec((tm, tn), lambda i, j: (i, j)),
        out_shape=jax.ShapeDtypeStruct((m, ncols), BF16),
        compiler_params=_params("arbitrary", "arbitrary"),
        name="in_proj",
    )(h, w, *rope)


def _branch_kernel(a_ref, b_ref, wa_ref, wb_ref, ga_ref, gb_ref, o_ref):
    wa = wa_ref[...].astype(BF16)
    wb = wb_ref[...].astype(BF16)
    for rows in _row_chunks(a_ref.shape[0], EPILOGUE_CHUNK_ROWS):
        oa = _bdot(a_ref[rows, :], wa)
        ob = _bdot(b_ref[rows, :], wb)
        o_ref[rows, :] = (ga_ref[rows, :].astype(F32) * oa
                          + gb_ref[rows, :].astype(F32) * ob).astype(o_ref.dtype)


def _branch_merge(attn, retg, w_a, w_b, layer, proj, gate_col0, tm, tn):
    m, ka = attn.shape
    kb = retg.shape[1]
    n = w_a.shape[2]
    tn = min(tn, n)
    nb = n // tn
    ja, jb = gate_col0 // tn, (gate_col0 + n) // tn
    return pl.pallas_call(
        _branch_kernel,
        grid=(m // tm, nb),
        in_specs=[
            _panel_spec(tm, ka),
            _panel_spec(tm, kb),
            pl.BlockSpec((None, ka, tn), lambda i, j: (layer, 0, j)),
            pl.BlockSpec((None, kb, tn), lambda i, j: (layer, 0, j)),
            pl.BlockSpec((tm, tn), lambda i, j: (i, j + ja)),
            pl.BlockSpec((tm, tn), lambda i, j: (i, j + jb)),
        ],
        out_specs=pl.BlockSpec((tm, tn), lambda i, j: (i, j)),
        out_shape=jax.ShapeDtypeStruct((m, n), BF16),
        compiler_params=_params("arbitrary", "arbitrary"),
        name="branch_merge",
    )(attn, retg, w_a, w_b, proj, proj)


PAIR = 2


def _sb_kernel(q_ref, k_ref, v_ref, o_ref, rhs_ref, mask_ref, *, ts, n_sub, n_first):
    @pl.when((pl.program_id(0) == 0) & (pl.program_id(1) == 0) & (pl.program_id(2) == 0))
    def _():
        row = lax.broadcasted_iota(jnp.int32, (ts, ts), 0)
        col = lax.broadcasted_iota(jnp.int32, (ts, ts), 1)
        suffix = jnp.concatenate([(row > col).astype(BF16), jnp.ones((ts, LANES), BF16)], axis=1)
        rhs_ref[...] = jnp.concatenate([suffix, suffix], axis=0)
        mask_ref[...] = (col < row).astype(F32)

    def weights_times_v(specs):
        starts = [pl.multiple_of(kb * ts, ts) for _, kb, _ in specs]
        zs = [lax.dot_general(q, k_ref[pl.ds(start, ts), :], (((1,), (1,)), ((), ())),
                              preferred_element_type=F32)
              for (q, _, _), start in zip(specs, starts)]
        log_betas, splits = [], []
        for (_, _, diagonal), z in zip(specs, zs):
            log_beta = jnp.minimum(z, 0.0) - jnp.log2(1.0 + jnp.exp2(-jnp.abs(z)))
            log_keep = log_beta - z
            if diagonal:
                log_keep = log_keep * mask_ref[...]
            hi = log_keep.astype(BF16)
            lo = (log_keep - hi.astype(F32)).astype(BF16)
            log_betas.append(log_beta)
            splits.append(jnp.concatenate([hi, lo], axis=1))
        sums = [_bdot(split, rhs_ref[...]) for split in splits]
        out = []
        for (_, _, diagonal), start, log_beta, s in zip(specs, starts, log_betas, sums):
            w = jnp.exp2(log_beta + s[:, :ts])
            if diagonal:
                w = w * mask_ref[...]
            out.append((_bdot(w.astype(BF16), v_ref[pl.ds(start, ts), :]), s[:, ts:]))
        return out

    def tile_specs(q, kb, count, diagonal):
        return [(q, jnp.maximum(kb - t, 0), diagonal and t == 0) for t in range(count)]

    def fold(results, kb, diagonal, carry, scale, acc):
        for t, (pv, total) in enumerate(results):
            if t > 0 or not diagonal:
                valid = jnp.where(kb - t >= 0, 1.0, 0.0)
                scale, total = scale * valid, total * valid
            acc = acc + scale * pv
            carry = carry + total
            scale = jnp.exp2(carry)
        return carry, scale, acc

    qs = [pl.program_id(2) * n_sub + sub for sub in range(n_sub)]
    q = [q_ref[sub * ts:(sub + 1) * ts, :] for sub in range(n_sub)]

    def advance(kbs, count, diagonal, states):
        results = weights_times_v([spec for sub in range(n_sub)
                                   for spec in tile_specs(q[sub], kbs[sub], count, diagonal)])
        live, new_states = 0.0, []
        for sub in range(n_sub):
            state = fold(results[sub * count:(sub + 1) * count], kbs[sub], diagonal, *states[sub])
            live = jnp.maximum(live, jnp.where(kbs[sub] - count >= 0, jnp.max(state[1]), 0.0))
            new_states.append(state)
        return live, tuple(new_states)

    zeros = jnp.zeros((ts, HEAD_DIM), F32)
    live, states = advance(qs, n_first, True, [(zeros, jnp.ones((ts, HEAD_DIM), F32), zeros)] * n_sub)

    def body(loop_state):
        it, _, states = loop_state
        kbs = [qs[sub] - n_first - PAIR * it for sub in range(n_sub)]
        return (it + 1,) + advance(kbs, PAIR, False, states)

    states = lax.while_loop(lambda loop_state: loop_state[1] > 0.0, body, (0, live, states))[2]
    for sub in range(n_sub):
        o_ref[sub * ts:(sub + 1) * ts, :] = states[sub][2].astype(o_ref.dtype)


def _stick_breaking(proj, batch, seq, tq, ts):
    m = proj.shape[0]
    nq = seq // tq
    return pl.pallas_call(
        functools.partial(_sb_kernel, ts=ts, n_sub=tq // ts, n_first=3),
        grid=(batch, N_HEADS, nq),
        in_specs=[
            pl.BlockSpec((tq, HEAD_DIM), lambda b, h, i: (b * nq + i, h)),
            pl.BlockSpec((seq, HEAD_DIM), lambda b, h, i: (b, N_HEADS + h)),
            pl.BlockSpec((seq, HEAD_DIM), lambda b, h, i: (b, 2 * N_HEADS + h)),
        ],
        out_specs=pl.BlockSpec((tq, HEAD_DIM), lambda b, h, i: (b * nq + i, h)),
        out_shape=jax.ShapeDtypeStruct((m, N_HEADS * HEAD_DIM), BF16),
        scratch_shapes=[pltpu.VMEM((2 * ts, ts + LANES), BF16), pltpu.VMEM((ts, ts), F32)],
        compiler_params=_params("arbitrary", "arbitrary", "arbitrary"),
        name="stick_breaking",
    )(proj, proj, proj)


def _ret_kernel(lg_ref, q_ref, k_ref, v_ref, gate_ref, gn_ref, o_ref, state_ref, entering_ref,
                *, chunk, n_chunks, scale):
    @pl.when(pl.program_id(2) == 0)
    def _():
        state_ref[...] = jnp.zeros_like(state_ref)

    lg = lg_ref[pl.program_id(1)]
    c = chunk
    row = lax.broadcasted_iota(jnp.int32, (c, c), 0).astype(F32)
    col = lax.broadcasted_iota(jnp.int32, (c, c), 1).astype(F32)
    diff = row - col
    decay = jnp.where(diff >= 0, jnp.exp(lg * jnp.maximum(diff, 0.0)), 0.0) * scale
    pos = lax.broadcasted_iota(jnp.int32, (c, HEAD_DIM), 0).astype(F32)
    q_decay = jnp.exp(lg * (pos + 1.0))
    k_decay = jnp.exp(lg * (c - 1.0 - pos)) * scale
    chunk_decay = jnp.exp(jnp.zeros((HEAD_DIM, HEAD_DIM), F32) + lg * c)
    gn = gn_ref[...]

    state = state_ref[...]
    for n in range(n_chunks):
        rows = slice(n * c, (n + 1) * c)
        entering_ref[n] = state.astype(BF16)
        kv = lax.dot_general((k_ref[rows, :].astype(F32) * k_decay).astype(BF16), v_ref[rows, :],
                             (((0,), (0,)), ((), ())), preferred_element_type=F32)
        state = chunk_decay * state + kv
    state_ref[...] = state

    for n in range(n_chunks):
        rows = slice(n * c, (n + 1) * c)
        q = q_ref[rows, :]
        scores = lax.dot_general(q, k_ref[rows, :], (((1,), (1,)), ((), ())),
                                 preferred_element_type=F32) * decay
        out = _bdot(scores.astype(BF16), v_ref[rows, :]) + q_decay * _bdot(q, entering_ref[n])
        mu = jnp.mean(out, axis=-1, keepdims=True)
        cen = out - mu
        var = jnp.mean(cen * cen, axis=-1, keepdims=True)
        normed = cen * lax.rsqrt(var + EPS) * gn
        g = gate_ref[rows, :].astype(F32)
        o_ref[rows, :] = (g * jax.nn.sigmoid(g) * normed).astype(o_ref.dtype)


def _retention(proj, head0, ret_gn, layer, batch, seq, rows_per_step):
    m = proj.shape[0]
    ns = seq // rows_per_step
    log_gamma = jnp.log1p(-jnp.exp2(-5.0 - jnp.arange(N_HEADS, dtype=F32)))
    blk = (rows_per_step, HEAD_DIM)

    def part(p):
        return pl.BlockSpec(blk, lambda b, h, s: (b * ns + s, head0 + p * N_HEADS + h))

    return pl.pallas_call(
        functools.partial(_ret_kernel, chunk=RET_CHUNK, n_chunks=rows_per_step // RET_CHUNK,
                          scale=HEAD_DIM ** -0.5),
        grid=(batch, N_HEADS, ns),
        in_specs=[
            pl.BlockSpec(memory_space=pltpu.SMEM),
            part(0), part(1), part(2), part(3),
            pl.BlockSpec((None, 1, HEAD_DIM), lambda b, h, s: (layer * N_HEADS + h, 0, 0)),
        ],
        out_specs=pl.BlockSpec(blk, lambda b, h, s: (b * ns + s, h)),
        out_shape=jax.ShapeDtypeStruct((m, N_HEADS * HEAD_DIM), BF16),
        scratch_shapes=[pltpu.VMEM((HEAD_DIM, HEAD_DIM), F32),
                        pltpu.VMEM((rows_per_step // RET_CHUNK, HEAD_DIM, HEAD_DIM), BF16)],
        compiler_params=_params("arbitrary", "arbitrary", "arbitrary"),
        name="retention",
    )(log_gamma, proj, proj, proj, proj, ret_gn.reshape(-1, 1, HEAD_DIM))


def _rope_tables(seq):
    half = HEAD_DIM // 2
    inv_freq = ROPE_BASE ** (-jnp.arange(half, dtype=F32) / half)
    ang = jnp.arange(seq, dtype=jnp.int32).astype(F32)[:, None] * inv_freq[None, :]
    cos, sin = jnp.cos(ang), jnp.sin(ang)
    return (jnp.concatenate([cos, cos], axis=-1), jnp.concatenate([-sin, sin], axis=-1))


def kernel(x, c, ada_down, ada_up, ada_bias, norm_ffn1, ffn1_in, ffn1_out, norm_mix, w_in, ret_gn,
           w_branch_a, w_branch_b, w_out, norm_ffn2, ffn2_in, ffn2_out, norm_final):
    batch, seq, d = x.shape
    depth = ada_down.shape[0]
    width = N_HEADS * HEAD_DIM
    tm_wide = min(2048, seq)
    tm_deep = min(1024, seq)
    tq = min(1024, seq)
    tn = 512
    rope = _rope_tables(seq)
    table = _ada_modulation(c, ada_down, ada_up, ada_bias)
    xf = x.reshape(batch * seq, d)
    for l in range(depth):
        mod = _Mod(table, l, batch, seq)
        h = _rms_norm(xf, norm_ffn1, l, seq, mod, 0, 1)
        xf = _resid_matmul(_ffn_in(h, ffn1_in, l, tm_wide), ffn1_out, l, xf, mod, 2, 0.5,
                           tm_deep, MXU_DIM)
        h = _rms_norm(xf, norm_mix, l, seq, mod, 3, 4)
        proj = _in_proj(h, w_in, l, seq, tm_wide, tn, rope, width, HEAD_DIM ** -0.5 * LOG2_E)
        attn = _stick_breaking(proj, batch, seq, tq, HEAD_DIM)
        retg = _retention(proj, 3 * N_HEADS, ret_gn, l, batch, seq, min(2048, seq))
        merged = _branch_merge(attn, retg, w_branch_a, w_branch_b, l, proj, 7 * width, tm_wide, tn)
        xf = _resid_matmul(merged, w_out, l, xf, mod, 5, 1.0, tm_wide, tn)
        h = _rms_norm(xf, norm_ffn2, l, seq, mod, 6, 7)
        xf = _resid_matmul(_ffn_in(h, ffn2_in, l, tm_wide), ffn2_out, l, xf, mod, 8, 0.5,
                           tm_deep, MXU_DIM)
    out = _rms_norm(xf, norm_final.reshape(1, d), 0, seq, out_dtype=x.dtype)
    return out.reshape(batch, seq, d)
```

```python
import functools
import math

import jax
import jax.numpy as jnp
from jax import lax
from jax.experimental import pallas as pl
from jax.experimental.pallas import tpu as pltpu

BF16 = jnp.bfloat16
F32 = jnp.float32

LANES = 128
MXU_DIM = 256
VMEM_LIMIT_BYTES = 56 * 1024 * 1024

HEAD_DIM = 128
N_HEADS = 16
N_MOD = 9
RET_CHUNK = 128
ROPE_BASE = 10000.0
EPS = 1e-6
LOG2_E = math.log2(math.e)


def _params(*semantics):
    return pltpu.CompilerParams(dimension_semantics=semantics,
                                vmem_limit_bytes=VMEM_LIMIT_BYTES)


def _bdot(a, b):
    return jnp.dot(a, b, preferred_element_type=F32)


def _panel_spec(tm, k):
    return pl.BlockSpec((tm, k), lambda i, j: (i, 0), pipeline_mode=pl.Buffered(1))


def _ada_kernel(c_ref, down_ref, up_ref, bias_ref, o_ref, t_ref):
    @pl.when(pl.program_id(1) == 0)
    def _():
        c = c_ref[...]
        t_ref[...] = jnp.dot(c * jax.nn.sigmoid(c), down_ref[...], preferred_element_type=F32,
                             precision=lax.Precision.HIGHEST)

    o_ref[...] = jnp.dot(t_ref[...], up_ref[...], preferred_element_type=F32,
                         precision=lax.Precision.HIGHEST) + bias_ref[...]


def _ada_modulation(c, ada_down, ada_up, ada_bias):
    depth, d, r = ada_down.shape
    n = ada_up.shape[2]
    b = c.shape[0]
    rows = 8
    c_pad = jnp.zeros((rows, d), F32).at[:b].set(c)
    tn = min(n, 4096)
    out = pl.pallas_call(
        _ada_kernel,
        grid=(depth, n // tn),
        in_specs=[
            pl.BlockSpec((rows, d), lambda l, j: (0, 0)),
            pl.BlockSpec((None, d, r), lambda l, j: (l, 0, 0)),
            pl.BlockSpec((None, r, tn), lambda l, j: (l, 0, j)),
            pl.BlockSpec((None, 1, tn), lambda l, j: (l, 0, j)),
        ],
        out_specs=pl.BlockSpec((None, rows, tn), lambda l, j: (l, 0, j)),
        out_shape=jax.ShapeDtypeStruct((depth, rows, n), F32),
        scratch_shapes=[pltpu.VMEM((rows, r), F32)],
        compiler_params=_params("arbitrary", "arbitrary"),
        name="ada_modulation",
    )(c_pad, ada_down, ada_up, ada_bias.reshape(depth, 1, n))
    return out[:, :b].reshape(depth * b * N_MOD, 1, d)


class _Mod:
    def __init__(self, table, layer, batch, seq):
        self.table, self.base, self.seq = table, layer * batch * N_MOD, seq

    def row(self, first_token, idx):
        return self.base + (first_token // self.seq) * N_MOD + idx


NORM_ROWS = 16
NORM_UNROLL = 4


def _norm_kernel(x_ref, g_ref, *rest, modulated):
    o_ref, r_ref = rest[-2], rest[-1]
    gain = g_ref[...]
    if modulated:
        sh_ref, sc_ref = rest[0], rest[1]
        gain = gain * (1.0 + sc_ref[...])
    trips = x_ref.shape[0] // NORM_ROWS

    def group(r):
        return pl.ds(pl.multiple_of(r * NORM_ROWS, NORM_ROWS), NORM_ROWS)

    def factors(r, carry):
        x = x_ref[group(r), :]
        r_ref[group(r), :] = jnp.zeros((NORM_ROWS, LANES), F32) + lax.rsqrt(
            jnp.mean(x * x, axis=-1, keepdims=True) + EPS)
        return carry

    def scale(r, carry):
        factor = r_ref[group(r), :]
        y = x_ref[group(r), :] * jnp.concatenate([factor] * (x_ref.shape[1] // LANES), axis=1) * gain
        if modulated:
            y = y + sh_ref[...]
        o_ref[group(r), :] = y.astype(o_ref.dtype)
        return carry

    lax.fori_loop(0, trips, factors, 0, unroll=NORM_UNROLL)
    lax.fori_loop(0, trips, scale, 0, unroll=NORM_UNROLL)


def _rms_norm(x, g, layer, seq, mod=None, shift_idx=0, scale_idx=0, out_dtype=None):
    m, d = x.shape
    tm = min(512, seq)
    in_specs = [pl.BlockSpec((tm, d), lambda i: (i, 0)),
                pl.BlockSpec((None, 1, d), lambda i: (layer, 0, 0))]
    args = [x, g.reshape(-1, 1, d)]
    if mod is not None:
        def mod_spec(idx):
            return pl.BlockSpec((None, 1, d), lambda i: (mod.row(i * tm, idx), 0, 0))
        in_specs += [mod_spec(shift_idx), mod_spec(scale_idx)]
        args += [mod.table, mod.table]
    return pl.pallas_call(
        functools.partial(_norm_kernel, modulated=mod is not None),
        grid=(m // tm,),
        in_specs=in_specs,
        out_specs=pl.BlockSpec((tm, d), lambda i: (i, 0)),
        out_shape=jax.ShapeDtypeStruct((m, d), out_dtype or BF16),
        scratch_shapes=[pltpu.VMEM((tm, LANES), F32)],
        compiler_params=_params("arbitrary"),
        name="rms_norm",
    )(*args)


EPILOGUE_CHUNK_ROWS = 512


def _row_chunks(tm, rows):
    rows = min(rows, tm)
    return [slice(r, r + rows) for r in range(0, tm, rows)]


def _ffn_in_kernel(h_ref, wa_ref, wb_ref, wout_ref, o_ref, wout_bf16_ref):
    @pl.when(pl.program_id(0) == 0)
    def _():
        wout_bf16_ref[...] = wout_ref[...].astype(BF16)

    wa = wa_ref[...].astype(BF16)
    wb = wb_ref[...].astype(BF16)
    for rows in _row_chunks(h_ref.shape[0], EPILOGUE_CHUNK_ROWS):
        h = h_ref[rows, :]
        a = _bdot(h, wa)
        b = _bdot(h, wb)
        o_ref[rows, :] = (a * jax.nn.sigmoid(a) * b).astype(o_ref.dtype)


def _ffn_in(h, w_in, w_out, layer, tm):
    m, d = h.shape
    f = w_in.shape[2] // 2
    n_out = w_out.shape[2]
    tn = MXU_DIM
    nb = f // tn

    def cast_block(i, j):
        return jnp.where(i == 0, j, nb - 1)

    return pl.pallas_call(
        _ffn_in_kernel,
        grid=(m // tm, nb),
        in_specs=[
            _panel_spec(tm, d),
            pl.BlockSpec((None, d, tn), lambda i, j: (layer, 0, j)),
            pl.BlockSpec((None, d, tn), lambda i, j: (layer, 0, j + nb)),
            pl.BlockSpec((None, tn, n_out), lambda i, j: (layer, cast_block(i, j), 0)),
        ],
        out_specs=[pl.BlockSpec((tm, tn), lambda i, j: (i, j)),
                   pl.BlockSpec((None, tn, n_out), lambda i, j: (0, cast_block(i, j), 0))],
        out_shape=[jax.ShapeDtypeStruct((m, f), BF16),
                   jax.ShapeDtypeStruct((1, f, n_out), BF16)],
        compiler_params=_params("arbitrary", "arbitrary"),
        name="ffn_in",
    )(h, w_in, w_in, w_out)


def _resid_kernel(a_ref, w_ref, x_ref, g_ref, o_ref, *, coef):
    w = w_ref[...].astype(BF16)
    gate = coef * g_ref[...]
    for rows in _row_chunks(a_ref.shape[0], EPILOGUE_CHUNK_ROWS):
        o_ref[rows, :] = x_ref[rows, :] + gate * _bdot(a_ref[rows, :], w)


def _resid_matmul(a, w, layer, x, mod, gate_idx, coef, tm, tn):
    m, k = a.shape
    n = w.shape[2]
    tn = min(tn, n)
    return pl.pallas_call(
        functools.partial(_resid_kernel, coef=coef),
        grid=(m // tm, n // tn),
        in_specs=[
            _panel_spec(tm, k),
            pl.BlockSpec((None, k, tn), lambda i, j: (layer, 0, j)),
            pl.BlockSpec((tm, tn), lambda i, j: (i, j)),
            pl.BlockSpec((None, 1, tn), lambda i, j: (mod.row(i * tm, gate_idx), 0, j)),
        ],
        out_specs=pl.BlockSpec((tm, tn), lambda i, j: (i, j)),
        out_shape=jax.ShapeDtypeStruct((m, n), F32),
        compiler_params=_params("arbitrary", "arbitrary"),
        name="resid_matmul",
    )(a, w, x, mod.table)


def _rotate_half_pairs(y, cos, sin_signed):
    parts = []
    for s in range(y.shape[1] // HEAD_DIM):
        blk = y[:, s * HEAD_DIM:(s + 1) * HEAD_DIM]
        parts.append(blk * cos + pltpu.roll(blk, HEAD_DIM // 2, 1) * sin_signed)
    return jnp.concatenate(parts, axis=1)


def _in_proj_kernel(h_ref, w_ref, cos_ref, sin_ref, o_ref, *, q_end, rot_start, rot_end,
                    gate_start, q_scale):
    j = pl.program_id(1)

    def run(act):
        w = w_ref[...].astype(BF16)
        for rows in _row_chunks(h_ref.shape[0], EPILOGUE_CHUNK_ROWS):
            y = _bdot(h_ref[rows, :], w)
            if act == "scale":
                y = y * q_scale
            elif act == "rotary":
                y = _rotate_half_pairs(y, cos_ref[rows, :], sin_ref[rows, :])
            elif act == "sigmoid":
                y = jax.nn.sigmoid(y)
            o_ref[rows, :] = y.astype(o_ref.dtype)

    rotary = (j >= rot_start) & (j < rot_end)
    pl.when(j < q_end)(lambda: run("scale"))
    pl.when(rotary)(lambda: run("rotary"))
    pl.when(j >= gate_start)(lambda: run("sigmoid"))
    pl.when((j >= q_end) & (j < gate_start) & jnp.logical_not(rotary))(lambda: run("none"))


def _in_proj(h, w, layer, seq, tm, tn, rope, width, q_scale):
    m, d = h.shape
    ncols = w.shape[2]
    spt = seq // tm
    return pl.pallas_call(
        functools.partial(_in_proj_kernel, q_end=width // tn, rot_start=3 * width // tn,
                          rot_end=5 * width // tn, gate_start=7 * width // tn, q_scale=q_scale),
        grid=(m // tm, ncols // tn),
        in_specs=[
            _panel_spec(tm, d),
            pl.BlockSpec((None, d, tn), lambda i, j: (layer, 0, j)),
            pl.BlockSpec((tm, HEAD_DIM), lambda i, j: (i % spt, 0)),
            pl.BlockSpec((tm, HEAD_DIM), lambda i, j: (i % spt, 0)),
        ],
        out_specs=pl.BlockSpec((tm, tn), lambda i, j: (i, j)),
        out_shape=jax.ShapeDtypeStruct((m, ncols), BF16),
        compiler_params=_params("arbitrary", "arbitrary"),
        name="in_proj",
    )(h, w, *rope)


def _branch_kernel(a_ref, b_ref, wa_ref, wb_ref, ga_ref, gb_ref, o_ref):
    wa = wa_ref[...].astype(BF16)
    wb = wb_ref[...].astype(BF16)
    for rows in _row_chunks(a_ref.shape[0], EPILOGUE_CHUNK_ROWS):
        oa = _bdot(a_ref[rows, :], wa)
        ob = _bdot(b_ref[rows, :], wb)
        o_ref[rows, :] = (ga_ref[rows, :].astype(F32) * oa
                          + gb_ref[rows, :].astype(F32) * ob).astype(o_ref.dtype)


def _branch_merge(attn, retg, w_a, w_b, layer, proj, gate_col0, tm, tn):
    m, ka = attn.shape
    kb = retg.shape[1]
    n = w_a.shape[2]
    tn = min(tn, n)
    nb = n // tn
    ja, jb = gate_col0 // tn, (gate_col0 + n) // tn
    return pl.pallas_call(
        _branch_kernel,
        grid=(m // tm, nb),
        in_specs=[
            _panel_spec(tm, ka),
            _panel_spec(tm, kb),
            pl.BlockSpec((None, ka, tn), lambda i, j: (layer, 0, j)),
            pl.BlockSpec((None, kb, tn), lambda i, j: (layer, 0, j)),
            pl.BlockSpec((tm, tn), lambda i, j: (i, j + ja)),
            pl.BlockSpec((tm, tn), lambda i, j: (i, j + jb)),
        ],
        out_specs=pl.BlockSpec((tm, tn), lambda i, j: (i, j)),
        out_shape=jax.ShapeDtypeStruct((m, n), BF16),
        compiler_params=_params("arbitrary", "arbitrary"),
        name="branch_merge",
    )(attn, retg, w_a, w_b, proj, proj)


PAIR = 2


def _sb_kernel(q_ref, k_ref, v_ref, o_ref, rhs_ref, mask_ref, *, ts, n_sub, n_first):
    @pl.when((pl.program_id(0) == 0) & (pl.program_id(1) == 0) & (pl.program_id(2) == 0))
    def _():
        row = lax.broadcasted_iota(jnp.int32, (ts, ts), 0)
        col = lax.broadcasted_iota(jnp.int32, (ts, ts), 1)
        suffix = jnp.concatenate([(row > col).astype(BF16), jnp.ones((ts, LANES), BF16)], axis=1)
        rhs_ref[...] = jnp.concatenate([suffix, suffix], axis=0)
        mask_ref[...] = (col < row).astype(F32)

    def weights_times_v(specs):
        starts = [pl.multiple_of(kb * ts, ts) for _, kb, _ in specs]
        zs = [lax.dot_general(q, k_ref[pl.ds(start, ts), :], (((1,), (1,)), ((), ())),
                              preferred_element_type=F32)
              for (q, _, _), start in zip(specs, starts)]
        log_betas, splits = [], []
        for (_, _, diagonal), z in zip(specs, zs):
            log_beta = jnp.minimum(z, 0.0) - jnp.log2(1.0 + jnp.exp2(-jnp.abs(z)))
            log_keep = log_beta - z
            if diagonal:
                log_keep = log_keep * mask_ref[...]
            hi = log_keep.astype(BF16)
            lo = (log_keep - hi.astype(F32)).astype(BF16)
            log_betas.append(log_beta)
            splits.append(jnp.concatenate([hi, lo], axis=1))
        sums = [_bdot(split, rhs_ref[...]) for split in splits]
        out = []
        for (_, _, diagonal), start, log_beta, s in zip(specs, starts, log_betas, sums):
            w = jnp.exp2(log_beta + s[:, :ts])
            if diagonal:
                w = w * mask_ref[...]
            out.append((_bdot(w.astype(BF16), v_ref[pl.ds(start, ts), :]), s[:, ts:]))
        return out

    def tile_specs(q, kb, count, diagonal):
        return [(q, jnp.maximum(kb - t, 0), diagonal and t == 0) for t in range(count)]

    def fold(results, kb, diagonal, carry, scale, acc):
        for t, (pv, total) in enumerate(results):
            if t > 0 or not diagonal:
                valid = jnp.where(kb - t >= 0, 1.0, 0.0)
                scale, total = scale * valid, total * valid
            acc = acc + scale * pv
            carry = carry + total
            scale = jnp.exp2(carry)
        return carry, scale, acc

    qs = [pl.program_id(2) * n_sub + sub for sub in range(n_sub)]
    q = [q_ref[sub * ts:(sub + 1) * ts, :] for sub in range(n_sub)]

    def advance(kbs, count, diagonal, states):
        results = weights_times_v([spec for sub in range(n_sub)
                                   for spec in tile_specs(q[sub], kbs[sub], count, diagonal)])
        live, new_states = 0.0, []
        for sub in range(n_sub):
            state = fold(results[sub * count:(sub + 1) * count], kbs[sub], diagonal, *states[sub])
            live = jnp.maximum(live, jnp.where(kbs[sub] - count >= 0, jnp.max(state[1]), 0.0))
            new_states.append(state)
        return live, tuple(new_states)

    zeros = jnp.zeros((ts, HEAD_DIM), F32)
    live, states = advance(qs, n_first, True, [(zeros, jnp.ones((ts, HEAD_DIM), F32), zeros)] * n_sub)

    def body(loop_state):
        it, _, states = loop_state
        kbs = [qs[sub] - n_first - PAIR * it for sub in range(n_sub)]
        return (it + 1,) + advance(kbs, PAIR, False, states)

    states = lax.while_loop(lambda loop_state: loop_state[1] > 0.0, body, (0, live, states))[2]
    for sub in range(n_sub):
        o_ref[sub * ts:(sub + 1) * ts, :] = states[sub][2].astype(o_ref.dtype)


def _stick_breaking(proj, batch, seq, tq, ts):
    m = proj.shape[0]
    nq = seq // tq
    return pl.pallas_call(
        functools.partial(_sb_kernel, ts=ts, n_sub=tq // ts, n_first=3),
        grid=(batch, N_HEADS, nq),
        in_specs=[
            pl.BlockSpec((tq, HEAD_DIM), lambda b, h, i: (b * nq + i, h)),
            pl.BlockSpec((seq, HEAD_DIM), lambda b, h, i: (b, N_HEADS + h)),
            pl.BlockSpec((seq, HEAD_DIM), lambda b, h, i: (b, 2 * N_HEADS + h)),
        ],
        out_specs=pl.BlockSpec((tq, HEAD_DIM), lambda b, h, i: (b * nq + i, h)),
        out_shape=jax.ShapeDtypeStruct((m, N_HEADS * HEAD_DIM), BF16),
        scratch_shapes=[pltpu.VMEM((2 * ts, ts + LANES), BF16), pltpu.VMEM((ts, ts), F32)],
        compiler_params=_params("arbitrary", "arbitrary", "arbitrary"),
        name="stick_breaking",
    )(proj, proj, proj)


def _ret_kernel(lg_ref, q_ref, k_ref, v_ref, gate_ref, gn_ref, o_ref, state_ref, entering_ref,
                *, chunk, n_chunks, scale):
    @pl.when(pl.program_id(2) == 0)
    def _():
        state_ref[...] = jnp.zeros_like(state_ref)

    lg = lg_ref[pl.program_id(1)]
    c = chunk
    row = lax.broadcasted_iota(jnp.int32, (c, c), 0).astype(F32)
    col = lax.broadcasted_iota(jnp.int32, (c, c), 1).astype(F32)
    diff = row - col
    decay = jnp.where(diff >= 0, jnp.exp(lg * jnp.maximum(diff, 0.0)), 0.0) * scale
    pos = lax.broadcasted_iota(jnp.int32, (c, HEAD_DIM), 0).astype(F32)
    q_decay = jnp.exp(lg * (pos + 1.0))
    k_decay = jnp.exp(lg * (c - 1.0 - pos)) * scale
    chunk_decay = jnp.exp(jnp.zeros((HEAD_DIM, HEAD_DIM), F32) + lg * c)
    gn = gn_ref[...]

    state = state_ref[...]
    for n in range(n_chunks):
        rows = slice(n * c, (n + 1) * c)
        entering_ref[n] = state.astype(BF16)
        kv = lax.dot_general((k_ref[rows, :].astype(F32) * k_decay).astype(BF16), v_ref[rows, :],
                             (((0,), (0,)), ((), ())), preferred_element_type=F32)
        state = chunk_decay * state + kv
    state_ref[...] = state

    for n in range(n_chunks):
        rows = slice(n * c, (n + 1) * c)
        q = q_ref[rows, :]
        scores = lax.dot_general(q, k_ref[rows, :], (((1,), (1,)), ((), ())),
                                 preferred_element_type=F32) * decay
        out = _bdot(scores.astype(BF16), v_ref[rows, :]) + q_decay * _bdot(q, entering_ref[n])
        mu = jnp.mean(out, axis=-1, keepdims=True)
        cen = out - mu
        var = jnp.mean(cen * cen, axis=-1, keepdims=True)
        normed = cen * lax.rsqrt(var + EPS) * gn
        g = gate_ref[rows, :].astype(F32)
        o_ref[rows, :] = (g * jax.nn.sigmoid(g) * normed).astype(o_ref.dtype)


def _retention(proj, head0, ret_gn, layer, batch, seq, rows_per_step):
    m = proj.shape[0]
    ns = seq // rows_per_step
    log_gamma = jnp.log1p(-jnp.exp2(-5.0 - jnp.arange(N_HEADS, dtype=F32)))
    blk = (rows_per_step, HEAD_DIM)

    def part(p):
        return pl.BlockSpec(blk, lambda b, h, s: (b * ns + s, head0 + p * N_HEADS + h))

    return pl.pallas_call(
        functools.partial(_ret_kernel, chunk=RET_CHUNK, n_chunks=rows_per_step // RET_CHUNK,
                          scale=HEAD_DIM ** -0.5),
        grid=(batch, N_HEADS, ns),
        in_specs=[
            pl.BlockSpec(memory_space=pltpu.SMEM),
            part(0), part(1), part(2), part(3),
            pl.BlockSpec((None, 1, HEAD_DIM), lambda b, h, s: (layer * N_HEADS + h, 0, 0)),
        ],
        out_specs=pl.BlockSpec(blk, lambda b, h, s: (b * ns + s, h)),
        out_shape=jax.ShapeDtypeStruct((m, N_HEADS * HEAD_DIM), BF16),
        scratch_shapes=[pltpu.VMEM((HEAD_DIM, HEAD_DIM), F32),
                        pltpu.VMEM((rows_per_step // RET_CHUNK, HEAD_DIM, HEAD_DIM), BF16)],
        compiler_params=_params("arbitrary", "arbitrary", "arbitrary"),
        name="retention",
    )(log_gamma, proj, proj, proj, proj, ret_gn.reshape(-1, 1, HEAD_DIM))


def _rope_tables(seq):
    half = HEAD_DIM // 2
    inv_freq = ROPE_BASE ** (-jnp.arange(half, dtype=F32) / half)
    ang = jnp.arange(seq, dtype=jnp.int32).astype(F32)[:, None] * inv_freq[None, :]
    cos, sin = jnp.cos(ang), jnp.sin(ang)
    return (jnp.concatenate([cos, cos], axis=-1), jnp.concatenate([-sin, sin], axis=-1))


def kernel(x, c, ada_down, ada_up, ada_bias, norm_ffn1, ffn1_in, ffn1_out, norm_mix, w_in, ret_gn,
           w_branch_a, w_branch_b, w_out, norm_ffn2, ffn2_in, ffn2_out, norm_final):
    batch, seq, d = x.shape
    depth = ada_down.shape[0]
    width = N_HEADS * HEAD_DIM
    tm_wide = min(2048, seq)
    tm_deep = min(1024, seq)
    tq = min(1024, seq)
    tn = 512
    rope = _rope_tables(seq)
    table = _ada_modulation(c, ada_down, ada_up, ada_bias)
    xf = x.reshape(batch * seq, d)
    for l in range(depth):
        mod = _Mod(table, l, batch, seq)
        h = _rms_norm(xf, norm_ffn1, l, seq, mod, 0, 1)
        act, w_bf16 = _ffn_in(h, ffn1_in, ffn1_out, l, tm_wide)
        xf = _resid_matmul(act, w_bf16, 0, xf, mod, 2, 0.5, tm_deep, tn)
        h = _rms_norm(xf, norm_mix, l, seq, mod, 3, 4)
        proj = _in_proj(h, w_in, l, seq, tm_wide, tn, rope, width, HEAD_DIM ** -0.5 * LOG2_E)
        attn = _stick_breaking(proj, batch, seq, tq, HEAD_DIM)
        retg = _retention(proj, 3 * N_HEADS, ret_gn, l, batch, seq, min(2048, seq))
        merged = _branch_merge(attn, retg, w_branch_a, w_branch_b, l, proj, 7 * width, tm_wide, tn)
        xf = _resid_matmul(merged, w_out, l, xf, mod, 5, 1.0, tm_wide, tn)
        h = _rms_norm(xf, norm_ffn2, l, seq, mod, 6, 7)
        act, w_bf16 = _ffn_in(h, ffn2_in, ffn2_out, l, tm_wide)
        xf = _resid_matmul(act, w_bf16, 0, xf, mod, 8, 0.5, tm_deep, tn)
    out = _rms_norm(xf, norm_final.reshape(1, d), 0, seq, out_dtype=x.dtype)
    return out.reshape(batch, seq, d)
```

```python
import functools
import math

import jax
import jax.numpy as jnp
from jax import lax
from jax.experimental import pallas as pl
from jax.experimental.pallas import tpu as pltpu

BF16 = jnp.bfloat16
F32 = jnp.float32

LANES = 128
MXU_DIM = 256
VMEM_LIMIT_BYTES = 56 * 1024 * 1024

HEAD_DIM = 128
N_HEADS = 16
N_MOD = 9
RET_CHUNK = 128
ROPE_BASE = 10000.0
EPS = 1e-6
LOG2_E = math.log2(math.e)


def _params(*semantics):
    return pltpu.CompilerParams(dimension_semantics=semantics,
                                vmem_limit_bytes=VMEM_LIMIT_BYTES)


def _bdot(a, b):
    return jnp.dot(a, b, preferred_element_type=F32)


def _panel_spec(tm, k):
    return pl.BlockSpec((tm, k), lambda i, j: (i, 0), pipeline_mode=pl.Buffered(1))


def _ada_kernel(c_ref, down_ref, up_ref, bias_ref, o_ref, t_ref):
    @pl.when(pl.program_id(1) == 0)
    def _():
        c = c_ref[...]
        t_ref[...] = jnp.dot(c * jax.nn.sigmoid(c), down_ref[...], preferred_element_type=F32,
                             precision=lax.Precision.HIGHEST)

    o_ref[...] = jnp.dot(t_ref[...], up_ref[...], preferred_element_type=F32,
                         precision=lax.Precision.HIGHEST) + bias_ref[...]


def _ada_modulation(c, ada_down, ada_up, ada_bias):
    depth, d, r = ada_down.shape
    n = ada_up.shape[2]
    b = c.shape[0]
    rows = 8
    c_pad = jnp.zeros((rows, d), F32).at[:b].set(c)
    tn = min(n, 4096)
    out = pl.pallas_call(
        _ada_kernel,
        grid=(depth, n // tn),
        in_specs=[
            pl.BlockSpec((rows, d), lambda l, j: (0, 0)),
            pl.BlockSpec((None, d, r), lambda l, j: (l, 0, 0)),
            pl.BlockSpec((None, r, tn), lambda l, j: (l, 0, j)),
            pl.BlockSpec((None, 1, tn), lambda l, j: (l, 0, j)),
        ],
        out_specs=pl.BlockSpec((None, rows, tn), lambda l, j: (l, 0, j)),
        out_shape=jax.ShapeDtypeStruct((depth, rows, n), F32),
        scratch_shapes=[pltpu.VMEM((rows, r), F32)],
        compiler_params=_params("arbitrary", "arbitrary"),
        name="ada_modulation",
    )(c_pad, ada_down, ada_up, ada_bias.reshape(depth, 1, n))
    return out[:, :b].reshape(depth * b * N_MOD, 1, d)


class _Mod:
    def __init__(self, table, layer, batch, seq):
        self.table, self.base, self.seq = table, layer * batch * N_MOD, seq

    def row(self, first_token, idx):
        return self.base + (first_token // self.seq) * N_MOD + idx


NORM_ROWS = 16
NORM_UNROLL = 4


def _norm_kernel(x_ref, g_ref, *rest, modulated):
    o_ref, r_ref = rest[-2], rest[-1]
    gain = g_ref[...]
    if modulated:
        sh_ref, sc_ref = rest[0], rest[1]
        gain = gain * (1.0 + sc_ref[...])
    trips = x_ref.shape[0] // NORM_ROWS

    def group(r):
        return pl.ds(pl.multiple_of(r * NORM_ROWS, NORM_ROWS), NORM_ROWS)

    def factors(r, carry):
        x = x_ref[group(r), :]
        r_ref[group(r), :] = jnp.zeros((NORM_ROWS, LANES), F32) + lax.rsqrt(
            jnp.mean(x * x, axis=-1, keepdims=True) + EPS)
        return carry

    def scale(r, carry):
        factor = r_ref[group(r), :]
        y = x_ref[group(r), :] * jnp.concatenate([factor] * (x_ref.shape[1] // LANES), axis=1) * gain
        if modulated:
            y = y + sh_ref[...]
        o_ref[group(r), :] = y.astype(o_ref.dtype)
        return carry

    lax.fori_loop(0, trips, factors, 0, unroll=NORM_UNROLL)
    lax.fori_loop(0, trips, scale, 0, unroll=NORM_UNROLL)


def _rms_norm(x, g, layer, seq, mod=None, shift_idx=0, scale_idx=0, out_dtype=None):
    m, d = x.shape
    tm = min(512, seq)
    in_specs = [pl.BlockSpec((tm, d), lambda i: (i, 0)),
                pl.BlockSpec((None, 1, d), lambda i: (layer, 0, 0))]
    args = [x, g.reshape(-1, 1, d)]
    if mod is not None:
        def mod_spec(idx):
            return pl.BlockSpec((None, 1, d), lambda i: (mod.row(i * tm, idx), 0, 0))
        in_specs += [mod_spec(shift_idx), mod_spec(scale_idx)]
        args += [mod.table, mod.table]
    return pl.pallas_call(
        functools.partial(_norm_kernel, modulated=mod is not None),
        grid=(m // tm,),
        in_specs=in_specs,
        out_specs=pl.BlockSpec((tm, d), lambda i: (i, 0)),
        out_shape=jax.ShapeDtypeStruct((m, d), out_dtype or BF16),
        scratch_shapes=[pltpu.VMEM((tm, LANES), F32)],
        compiler_params=_params("arbitrary"),
        name="rms_norm",
    )(*args)


EPILOGUE_CHUNK_ROWS = 512


def _row_chunks(tm, rows):
    rows = min(rows, tm)
    return [slice(r, r + rows) for r in range(0, tm, rows)]


def _ffn_in_kernel(h_ref, wa_ref, wb_ref, wout_ref, o_ref, wout_bf16_ref):
    @pl.when(pl.program_id(0) == 0)
    def _():
        wout_bf16_ref[...] = wout_ref[...].astype(BF16)

    wa = wa_ref[...].astype(BF16)
    wb = wb_ref[...].astype(BF16)
    for rows in _row_chunks(h_ref.shape[0], EPILOGUE_CHUNK_ROWS):
        h = h_ref[rows, :]
        a = _bdot(h, wa)
        b = _bdot(h, wb)
        o_ref[rows, :] = (a * jax.nn.sigmoid(a) * b).astype(o_ref.dtype)


def _ffn_in(h, w_in, w_out, layer, tm):
    m, d = h.shape
    f = w_in.shape[2] // 2
    n_out = w_out.shape[2]
    tn = MXU_DIM
    nb = f // tn

    def cast_block(i, j):
        return jnp.where(i == 0, j, nb - 1)

    return pl.pallas_call(
        _ffn_in_kernel,
        grid=(m // tm, nb),
        in_specs=[
            _panel_spec(tm, d),
            pl.BlockSpec((None, d, tn), lambda i, j: (layer, 0, j)),
            pl.BlockSpec((None, d, tn), lambda i, j: (layer, 0, j + nb)),
            pl.BlockSpec((None, tn, n_out), lambda i, j: (layer, cast_block(i, j), 0)),
        ],
        out_specs=[pl.BlockSpec((tm, tn), lambda i, j: (i, j)),
                   pl.BlockSpec((None, tn, n_out), lambda i, j: (0, cast_block(i, j), 0))],
        out_shape=[jax.ShapeDtypeStruct((m, f), BF16),
                   jax.ShapeDtypeStruct((1, f, n_out), BF16)],
        compiler_params=_params("arbitrary", "arbitrary"),
        name="ffn_in",
    )(h, w_in, w_in, w_out)


def _resid_kernel(a_ref, w_ref, x_ref, g_ref, o_ref, *, coef):
    w = w_ref[...].astype(BF16)
    gate = coef * g_ref[...]
    for rows in _row_chunks(a_ref.shape[0], EPILOGUE_CHUNK_ROWS):
        o_ref[rows, :] = x_ref[rows, :] + gate * _bdot(a_ref[rows, :], w)


def _resid_matmul(a, w, layer, x, mod, gate_idx, coef, tm, tn, resident_panel=True):
    m, k = a.shape
    n = w.shape[2]
    tn = min(tn, n)
    return pl.pallas_call(
        functools.partial(_resid_kernel, coef=coef),
        grid=(m // tm, n // tn),
        in_specs=[
            _panel_spec(tm, k) if resident_panel else pl.BlockSpec((tm, k), lambda i, j: (i, 0)),
            pl.BlockSpec((None, k, tn), lambda i, j: (layer, 0, j)),
            pl.BlockSpec((tm, tn), lambda i, j: (i, j)),
            pl.BlockSpec((None, 1, tn), lambda i, j: (mod.row(i * tm, gate_idx), 0, j)),
        ],
        out_specs=pl.BlockSpec((tm, tn), lambda i, j: (i, j)),
        out_shape=jax.ShapeDtypeStruct((m, n), F32),
        compiler_params=_params("arbitrary", "arbitrary"),
        name="resid_matmul",
    )(a, w, x, mod.table)


def _rotate_half_pairs(y, cos, sin_signed):
    parts = []
    for s in range(y.shape[1] // HEAD_DIM):
        blk = y[:, s * HEAD_DIM:(s + 1) * HEAD_DIM]
        parts.append(blk * cos + pltpu.roll(blk, HEAD_DIM // 2, 1) * sin_signed)
    return jnp.concatenate(parts, axis=1)


def _in_proj_kernel(h_ref, w_ref, cos_ref, sin_ref, o_ref, *, q_end, rot_start, rot_end,
                    gate_start, q_scale):
    j = pl.program_id(1)

    def run(act):
        w = w_ref[...].astype(BF16)
        for rows in _row_chunks(h_ref.shape[0], EPILOGUE_CHUNK_ROWS):
            y = _bdot(h_ref[rows, :], w)
            if act == "scale":
                y = y * q_scale
            elif act == "rotary":
                y = _rotate_half_pairs(y, cos_ref[rows, :], sin_ref[rows, :])
            elif act == "sigmoid":
                y = jax.nn.sigmoid(y)
            o_ref[rows, :] = y.astype(o_ref.dtype)

    rotary = (j >= rot_start) & (j < rot_end)
    pl.when(j < q_end)(lambda: run("scale"))
    pl.when(rotary)(lambda: run("rotary"))
    pl.when(j >= gate_start)(lambda: run("sigmoid"))
    pl.when((j >= q_end) & (j < gate_start) & jnp.logical_not(rotary))(lambda: run("none"))


def _in_proj(h, w, layer, seq, tm, tn, rope, width, q_scale):
    m, d = h.shape
    ncols = w.shape[2]
    spt = seq // tm
    return pl.pallas_call(
        functools.partial(_in_proj_kernel, q_end=width // tn, rot_start=3 * width // tn,
                          rot_end=5 * width // tn, gate_start=7 * width // tn, q_scale=q_scale),
        grid=(m // tm, ncols // tn),
        in_specs=[
            _panel_spec(tm, d),
            pl.BlockSpec((None, d, tn), lambda i, j: (layer, 0, j)),
            pl.BlockSpec((tm, HEAD_DIM), lambda i, j: (i % spt, 0)),
            pl.BlockSpec((tm, HEAD_DIM), lambda i, j: (i % spt, 0)),
        ],
        out_specs=pl.BlockSpec((tm, tn), lambda i, j: (i, j)),
        out_shape=jax.ShapeDtypeStruct((m, ncols), BF16),
        compiler_params=_params("arbitrary", "arbitrary"),
        name="in_proj",
    )(h, w, *rope)


def _branch_kernel(a_ref, b_ref, wa_ref, wb_ref, ga_ref, gb_ref, o_ref):
    wa = wa_ref[...].astype(BF16)
    wb = wb_ref[...].astype(BF16)
    for rows in _row_chunks(a_ref.shape[0], EPILOGUE_CHUNK_ROWS):
        oa = _bdot(a_ref[rows, :], wa)
        ob = _bdot(b_ref[rows, :], wb)
        o_ref[rows, :] = (ga_ref[rows, :].astype(F32) * oa
                          + gb_ref[rows, :].astype(F32) * ob).astype(o_ref.dtype)


def _branch_merge(attn, retg, w_a, w_b, layer, proj, gate_col0, tm, tn):
    m, ka = attn.shape
    kb = retg.shape[1]
    n = w_a.shape[2]
    tn = min(tn, n)
    nb = n // tn
    ja, jb = gate_col0 // tn, (gate_col0 + n) // tn
    return pl.pallas_call(
        _branch_kernel,
        grid=(m // tm, nb),
        in_specs=[
            _panel_spec(tm, ka),
            _panel_spec(tm, kb),
            pl.BlockSpec((None, ka, tn), lambda i, j: (layer, 0, j)),
            pl.BlockSpec((None, kb, tn), lambda i, j: (layer, 0, j)),
            pl.BlockSpec((tm, tn), lambda i, j: (i, j + ja)),
            pl.BlockSpec((tm, tn), lambda i, j: (i, j + jb)),
        ],
        out_specs=pl.BlockSpec((tm, tn), lambda i, j: (i, j)),
        out_shape=jax.ShapeDtypeStruct((m, n), BF16),
        compiler_params=_params("arbitrary", "arbitrary"),
        name="branch_merge",
    )(attn, retg, w_a, w_b, proj, proj)


PAIR = 2


def _sb_kernel(q_ref, k_ref, v_ref, o_ref, rhs_ref, mask_ref, *, ts, n_sub, n_first):
    @pl.when((pl.program_id(0) == 0) & (pl.program_id(1) == 0) & (pl.program_id(2) == 0))
    def _():
        row = lax.broadcasted_iota(jnp.int32, (ts, ts), 0)
        col = lax.broadcasted_iota(jnp.int32, (ts, ts), 1)
        suffix = jnp.concatenate([(row > col).astype(BF16), jnp.ones((ts, LANES), BF16)], axis=1)
        rhs_ref[...] = jnp.concatenate([suffix, suffix], axis=0)
        mask_ref[...] = (col < row).astype(F32)

    def weights_times_v(specs):
        starts = [pl.multiple_of(kb * ts, ts) for _, kb, _ in specs]
        zs = [lax.dot_general(q, k_ref[pl.ds(start, ts), :], (((1,), (1,)), ((), ())),
                              preferred_element_type=F32)
              for (q, _, _), start in zip(specs, starts)]
        log_betas, splits = [], []
        for (_, _, diagonal), z in zip(specs, zs):
            log_beta = jnp.minimum(z, 0.0) - jnp.log2(1.0 + jnp.exp2(-jnp.abs(z)))
            log_keep = log_beta - z
            if diagonal:
                log_keep = log_keep * mask_ref[...]
            hi = log_keep.astype(BF16)
            lo = (log_keep - hi.astype(F32)).astype(BF16)
            log_betas.append(log_beta)
            splits.append(jnp.concatenate([hi, lo], axis=1))
        sums = [_bdot(split, rhs_ref[...]) for split in splits]
        out = []
        for (_, _, diagonal), start, log_beta, s in zip(specs, starts, log_betas, sums):
            w = jnp.exp2(log_beta + s[:, :ts])
            if diagonal:
                w = w * mask_ref[...]
            out.append((_bdot(w.astype(BF16), v_ref[pl.ds(start, ts), :]), s[:, ts:]))
        return out

    def tile_specs(q, kb, count, diagonal):
        return [(q, jnp.maximum(kb - t, 0), diagonal and t == 0) for t in range(count)]

    def fold(results, kb, diagonal, carry, scale, acc):
        for t, (pv, total) in enumerate(results):
            if t > 0 or not diagonal:
                valid = jnp.where(kb - t >= 0, 1.0, 0.0)
                scale, total = scale * valid, total * valid
            acc = acc + scale * pv
            carry = carry + total
            scale = jnp.exp2(carry)
        return carry, scale, acc

    qs = [pl.program_id(2) * n_sub + sub for sub in range(n_sub)]
    q = [q_ref[sub * ts:(sub + 1) * ts, :] for sub in range(n_sub)]

    def advance(kbs, count, diagonal, states):
        results = weights_times_v([spec for sub in range(n_sub)
                                   for spec in tile_specs(q[sub], kbs[sub], count, diagonal)])
        live, new_states = 0.0, []
        for sub in range(n_sub):
            state = fold(results[sub * count:(sub + 1) * count], kbs[sub], diagonal, *states[sub])
            live = jnp.maximum(live, jnp.where(kbs[sub] - count >= 0, jnp.max(state[1]), 0.0))
            new_states.append(state)
        return live, tuple(new_states)

    zeros = jnp.zeros((ts, HEAD_DIM), F32)
    live, states = advance(qs, n_first, True, [(zeros, jnp.ones((ts, HEAD_DIM), F32), zeros)] * n_sub)

    def body(loop_state):
        it, _, states = loop_state
        kbs = [qs[sub] - n_first - PAIR * it for sub in range(n_sub)]
        return (it + 1,) + advance(kbs, PAIR, False, states)

    states = lax.while_loop(lambda loop_state: loop_state[1] > 0.0, body, (0, live, states))[2]
    for sub in range(n_sub):
        o_ref[sub * ts:(sub + 1) * ts, :] = states[sub][2].astype(o_ref.dtype)


def _stick_breaking(proj, batch, seq, tq, ts):
    m = proj.shape[0]
    nq = seq // tq
    return pl.pallas_call(
        functools.partial(_sb_kernel, ts=ts, n_sub=tq // ts, n_first=3),
        grid=(batch, N_HEADS, nq),
        in_specs=[
            pl.BlockSpec((tq, HEAD_DIM), lambda b, h, i: (b * nq + i, h)),
            pl.BlockSpec((seq, HEAD_DIM), lambda b, h, i: (b, N_HEADS + h)),
            pl.BlockSpec((seq, HEAD_DIM), lambda b, h, i: (b, 2 * N_HEADS + h)),
        ],
        out_specs=pl.BlockSpec((tq, HEAD_DIM), lambda b, h, i: (b * nq + i, h)),
        out_shape=jax.ShapeDtypeStruct((m, N_HEADS * HEAD_DIM), BF16),
        scratch_shapes=[pltpu.VMEM((2 * ts, ts + LANES), BF16), pltpu.VMEM((ts, ts), F32)],
        compiler_params=_params("arbitrary", "arbitrary", "arbitrary"),
        name="stick_breaking",
    )(proj, proj, proj)


def _ret_kernel(lg_ref, q_ref, k_ref, v_ref, gate_ref, gn_ref, o_ref, state_ref, entering_ref,
                *, chunk, n_chunks, scale):
    @pl.when(pl.program_id(2) == 0)
    def _():
        state_ref[...] = jnp.zeros_like(state_ref)

    lg = lg_ref[pl.program_id(1)]
    c = chunk
    row = lax.broadcasted_iota(jnp.int32, (c, c), 0).astype(F32)
    col = lax.broadcasted_iota(jnp.int32, (c, c), 1).astype(F32)
    diff = row - col
    decay = jnp.where(diff >= 0, jnp.exp(lg * jnp.maximum(diff, 0.0)), 0.0) * scale
    pos = lax.broadcasted_iota(jnp.int32, (c, HEAD_DIM), 0).astype(F32)
    q_decay = jnp.exp(lg * (pos + 1.0))
    k_decay = jnp.exp(lg * (c - 1.0 - pos)) * scale
    chunk_decay = jnp.exp(jnp.zeros((HEAD_DIM, HEAD_DIM), F32) + lg * c)
    gn = gn_ref[...]

    state = state_ref[...]
    for n in range(n_chunks):
        rows = slice(n * c, (n + 1) * c)
        entering_ref[n] = state.astype(BF16)
        kv = lax.dot_general((k_ref[rows, :].astype(F32) * k_decay).astype(BF16), v_ref[rows, :],
                             (((0,), (0,)), ((), ())), preferred_element_type=F32)
        state = chunk_decay * state + kv
    state_ref[...] = state

    for n in range(n_chunks):
        rows = slice(n * c, (n + 1) * c)
        q = q_ref[rows, :]
        scores = lax.dot_general(q, k_ref[rows, :], (((1,), (1,)), ((), ())),
                                 preferred_element_type=F32) * decay
        out = _bdot(scores.astype(BF16), v_ref[rows, :]) + q_decay * _bdot(q, entering_ref[n])
        mu = jnp.mean(out, axis=-1, keepdims=True)
        cen = out - mu
        var = jnp.mean(cen * cen, axis=-1, keepdims=True)
        normed = cen * lax.rsqrt(var + EPS) * gn
        g = gate_ref[rows, :].astype(F32)
        o_ref[rows, :] = (g * jax.nn.sigmoid(g) * normed).astype(o_ref.dtype)


def _retention(proj, head0, ret_gn, layer, batch, seq, rows_per_step):
    m = proj.shape[0]
    ns = seq // rows_per_step
    log_gamma = jnp.log1p(-jnp.exp2(-5.0 - jnp.arange(N_HEADS, dtype=F32)))
    blk = (rows_per_step, HEAD_DIM)

    def part(p):
        return pl.BlockSpec(blk, lambda b, h, s: (b * ns + s, head0 + p * N_HEADS + h))

    return pl.pallas_call(
        functools.partial(_ret_kernel, chunk=RET_CHUNK, n_chunks=rows_per_step // RET_CHUNK,
                          scale=HEAD_DIM ** -0.5),
        grid=(batch, N_HEADS, ns),
        in_specs=[
            pl.BlockSpec(memory_space=pltpu.SMEM),
            part(0), part(1), part(2), part(3),
            pl.BlockSpec((None, 1, HEAD_DIM), lambda b, h, s: (layer * N_HEADS + h, 0, 0)),
        ],
        out_specs=pl.BlockSpec(blk, lambda b, h, s: (b * ns + s, h)),
        out_shape=jax.ShapeDtypeStruct((m, N_HEADS * HEAD_DIM), BF16),
        scratch_shapes=[pltpu.VMEM((HEAD_DIM, HEAD_DIM), F32),
                        pltpu.VMEM((rows_per_step // RET_CHUNK, HEAD_DIM, HEAD_DIM), BF16)],
        compiler_params=_params("arbitrary", "arbitrary", "arbitrary"),
        name="retention",
    )(log_gamma, proj, proj, proj, proj, ret_gn.reshape(-1, 1, HEAD_DIM))


def _rope_tables(seq):
    half = HEAD_DIM // 2
    inv_freq = ROPE_BASE ** (-jnp.arange(half, dtype=F32) / half)
    ang = jnp.arange(seq, dtype=jnp.int32).astype(F32)[:, None] * inv_freq[None, :]
    cos, sin = jnp.cos(ang), jnp.sin(ang)
    return (jnp.concatenate([cos, cos], axis=-1), jnp.concatenate([-sin, sin], axis=-1))


def kernel(x, c, ada_down, ada_up, ada_bias, norm_ffn1, ffn1_in, ffn1_out, norm_mix, w_in, ret_gn,
           w_branch_a, w_branch_b, w_out, norm_ffn2, ffn2_in, ffn2_out, norm_final):
    batch, seq, d = x.shape
    depth = ada_down.shape[0]
    width = N_HEADS * HEAD_DIM
    tm_wide = min(2048, seq)
    tm_deep = min(512, seq)
    tq = min(1024, seq)
    tn = 512
    rope = _rope_tables(seq)
    table = _ada_modulation(c, ada_down, ada_up, ada_bias)
    xf = x.reshape(batch * seq, d)
    for l in range(depth):
        mod = _Mod(table, l, batch, seq)
        h = _rms_norm(xf, norm_ffn1, l, seq, mod, 0, 1)
        act, w_bf16 = _ffn_in(h, ffn1_in, ffn1_out, l, tm_wide)
        xf = _resid_matmul(act, w_bf16, 0, xf, mod, 2, 0.5, tm_deep, tn, resident_panel=False)
        h = _rms_norm(xf, norm_mix, l, seq, mod, 3, 4)
        proj = _in_proj(h, w_in, l, seq, tm_wide, tn, rope, width, HEAD_DIM ** -0.5 * LOG2_E)
        attn = _stick_breaking(proj, batch, seq, tq, HEAD_DIM)
        retg = _retention(proj, 3 * N_HEADS, ret_gn, l, batch, seq, min(2048, seq))
        merged = _branch_merge(attn, retg, w_branch_a, w_branch_b, l, proj, 7 * width, tm_wide, tn)
        xf = _resid_matmul(merged, w_out, l, xf, mod, 5, 1.0, tm_wide, tn)
        h = _rms_norm(xf, norm_ffn2, l, seq, mod, 6, 7)
        act, w_bf16 = _ffn_in(h, ffn2_in, ffn2_out, l, tm_wide)
        xf = _resid_matmul(act, w_bf16, 0, xf, mod, 8, 0.5, tm_deep, tn, resident_panel=False)
    out = _rms_norm(xf, norm_final.reshape(1, d), 0, seq, out_dtype=x.dtype)
    return out.reshape(batch, seq, d)
```

```python
import functools
import math

import jax
import jax.numpy as jnp
from jax import lax
from jax.experimental import pallas as pl
from jax.experimental.pallas import tpu as pltpu

BF16 = jnp.bfloat16
F32 = jnp.float32

LANES = 128
MXU_DIM = 256
VMEM_LIMIT_BYTES = 56 * 1024 * 1024

HEAD_DIM = 128
N_HEADS = 16
N_MOD = 9
RET_CHUNK = 128
ROPE_BASE = 10000.0
EPS = 1e-6
LOG2_E = math.log2(math.e)


def _params(*semantics):
    return pltpu.CompilerParams(dimension_semantics=semantics,
                                vmem_limit_bytes=VMEM_LIMIT_BYTES)


def _bdot(a, b):
    return jnp.dot(a, b, preferred_element_type=F32)


def _panel_spec(tm, k):
    return pl.BlockSpec((tm, k), lambda i, j: (i, 0), pipeline_mode=pl.Buffered(1))


def _ada_kernel(c_ref, down_ref, up_ref, bias_ref, o_ref, t_ref):
    @pl.when(pl.program_id(1) == 0)
    def _():
        c = c_ref[...]
        t_ref[...] = jnp.dot(c * jax.nn.sigmoid(c), down_ref[...], preferred_element_type=F32,
                             precision=lax.Precision.HIGHEST)

    o_ref[...] = jnp.dot(t_ref[...], up_ref[...], preferred_element_type=F32,
                         precision=lax.Precision.HIGHEST) + bias_ref[...]


def _ada_modulation(c, ada_down, ada_up, ada_bias):
    depth, d, r = ada_down.shape
    n = ada_up.shape[2]
    b = c.shape[0]
    rows = 8
    c_pad = jnp.zeros((rows, d), F32).at[:b].set(c)
    tn = min(n, 4096)
    out = pl.pallas_call(
        _ada_kernel,
        grid=(depth, n // tn),
        in_specs=[
            pl.BlockSpec((rows, d), lambda l, j: (0, 0)),
            pl.BlockSpec((None, d, r), lambda l, j: (l, 0, 0)),
            pl.BlockSpec((None, r, tn), lambda l, j: (l, 0, j)),
            pl.BlockSpec((None, 1, tn), lambda l, j: (l, 0, j)),
        ],
        out_specs=pl.BlockSpec((None, rows, tn), lambda l, j: (l, 0, j)),
        out_shape=jax.ShapeDtypeStruct((depth, rows, n), F32),
        scratch_shapes=[pltpu.VMEM((rows, r), F32)],
        compiler_params=_params("arbitrary", "arbitrary"),
        name="ada_modulation",
    )(c_pad, ada_down, ada_up, ada_bias.reshape(depth, 1, n))
    return out[:, :b].reshape(depth * b * N_MOD, 1, d)


class _Mod:
    def __init__(self, table, layer, batch, seq):
        self.table, self.base, self.seq = table, layer * batch * N_MOD, seq

    def row(self, first_token, idx):
        return self.base + (first_token // self.seq) * N_MOD + idx


NORM_ROWS = 16
NORM_UNROLL = 4


def _norm_kernel(x_ref, g_ref, *rest, modulated):
    o_ref, r_ref = rest[-2], rest[-1]
    gain = g_ref[...]
    if modulated:
        sh_ref, sc_ref = rest[0], rest[1]
        gain = gain * (1.0 + sc_ref[...])
    trips = x_ref.shape[0] // NORM_ROWS

    def group(r):
        return pl.ds(pl.multiple_of(r * NORM_ROWS, NORM_ROWS), NORM_ROWS)

    def factors(r, carry):
        x = x_ref[group(r), :]
        r_ref[group(r), :] = jnp.zeros((NORM_ROWS, LANES), F32) + lax.rsqrt(
            jnp.mean(x * x, axis=-1, keepdims=True) + EPS)
        return carry

    def scale(r, carry):
        factor = r_ref[group(r), :]
        y = x_ref[group(r), :] * jnp.concatenate([factor] * (x_ref.shape[1] // LANES), axis=1) * gain
        if modulated:
            y = y + sh_ref[...]
        o_ref[group(r), :] = y.astype(o_ref.dtype)
        return carry

    lax.fori_loop(0, trips, factors, 0, unroll=NORM_UNROLL)
    lax.fori_loop(0, trips, scale, 0, unroll=NORM_UNROLL)


def _rms_norm(x, g, layer, seq, mod=None, shift_idx=0, scale_idx=0, out_dtype=None):
    m, d = x.shape
    tm = min(512, seq)
    in_specs = [pl.BlockSpec((tm, d), lambda i: (i, 0)),
                pl.BlockSpec((None, 1, d), lambda i: (layer, 0, 0))]
    args = [x, g.reshape(-1, 1, d)]
    if mod is not None:
        def mod_spec(idx):
            return pl.BlockSpec((None, 1, d), lambda i: (mod.row(i * tm, idx), 0, 0))
        in_specs += [mod_spec(shift_idx), mod_spec(scale_idx)]
        args += [mod.table, mod.table]
    return pl.pallas_call(
        functools.partial(_norm_kernel, modulated=mod is not None),
        grid=(m // tm,),
        in_specs=in_specs,
        out_specs=pl.BlockSpec((tm, d), lambda i: (i, 0)),
        out_shape=jax.ShapeDtypeStruct((m, d), out_dtype or BF16),
        scratch_shapes=[pltpu.VMEM((tm, LANES), F32)],
        compiler_params=_params("arbitrary"),
        name="rms_norm",
    )(*args)


EPILOGUE_CHUNK_ROWS = 256


def _row_chunks(tm, rows):
    rows = min(rows, tm)
    return [slice(r, r + rows) for r in range(0, tm, rows)]


def _ffn_in_kernel(h_ref, wa_ref, wb_ref, wout_ref, o_ref, wout_bf16_ref):
    wout_bf16_ref[...] = wout_ref[...].astype(BF16)
    wa = wa_ref[...].astype(BF16)
    wb = wb_ref[...].astype(BF16)
    for rows in _row_chunks(h_ref.shape[0], EPILOGUE_CHUNK_ROWS):
        h = h_ref[rows, :]
        a = _bdot(h, wa)
        b = _bdot(h, wb)
        o_ref[rows, :] = (a * jax.nn.sigmoid(a) * b).astype(o_ref.dtype)


def _ffn_in(h, w_in, w_out, layer, tm):
    m, d = h.shape
    f = w_in.shape[2] // 2
    n_out = w_out.shape[2]
    tn = MXU_DIM
    nb = f // tn
    cast_rows = f // (m // tm * nb)
    assert cast_rows * (m // tm * nb) == f and cast_rows % 16 == 0, (f, m // tm, nb)
    return pl.pallas_call(
        _ffn_in_kernel,
        grid=(m // tm, nb),
        in_specs=[
            _panel_spec(tm, d),
            pl.BlockSpec((None, d, tn), lambda i, j: (layer, 0, j)),
            pl.BlockSpec((None, d, tn), lambda i, j: (layer, 0, j + nb)),
            pl.BlockSpec((None, cast_rows, n_out), lambda i, j: (layer, i * nb + j, 0)),
        ],
        out_specs=[pl.BlockSpec((tm, tn), lambda i, j: (i, j)),
                   pl.BlockSpec((None, cast_rows, n_out), lambda i, j: (0, i * nb + j, 0))],
        out_shape=[jax.ShapeDtypeStruct((m, f), BF16),
                   jax.ShapeDtypeStruct((1, f, n_out), BF16)],
        compiler_params=_params("arbitrary", "arbitrary"),
        name="ffn_in",
    )(h, w_in, w_in, w_out)


def _resid_kernel(a_ref, w_ref, x_ref, g_ref, o_ref, *, coef):
    w = w_ref[...].astype(BF16)
    gate = coef * g_ref[...]
    for rows in _row_chunks(a_ref.shape[0], EPILOGUE_CHUNK_ROWS):
        o_ref[rows, :] = x_ref[rows, :] + gate * _bdot(a_ref[rows, :], w)


def _resid_matmul(a, w, layer, x, mod, gate_idx, coef, tm, tn, resident_panel=True):
    m, k = a.shape
    n = w.shape[2]
    tn = min(tn, n)
    return pl.pallas_call(
        functools.partial(_resid_kernel, coef=coef),
        grid=(m // tm, n // tn),
        in_specs=[
            _panel_spec(tm, k) if resident_panel else pl.BlockSpec((tm, k), lambda i, j: (i, 0)),
            pl.BlockSpec((None, k, tn), lambda i, j: (layer, 0, j)),
            pl.BlockSpec((tm, tn), lambda i, j: (i, j)),
            pl.BlockSpec((None, 1, tn), lambda i, j: (mod.row(i * tm, gate_idx), 0, j)),
        ],
        out_specs=pl.BlockSpec((tm, tn), lambda i, j: (i, j)),
        out_shape=jax.ShapeDtypeStruct((m, n), F32),
        compiler_params=_params("arbitrary", "arbitrary"),
        name="resid_matmul",
    )(a, w, x, mod.table)


def _rotate_half_pairs(y, cos, sin_signed):
    parts = []
    for s in range(y.shape[1] // HEAD_DIM):
        blk = y[:, s * HEAD_DIM:(s + 1) * HEAD_DIM]
        parts.append(blk * cos + pltpu.roll(blk, HEAD_DIM // 2, 1) * sin_signed)
    return jnp.concatenate(parts, axis=1)


def _in_proj_kernel(h_ref, w_ref, cos_ref, sin_ref, o_ref, *, q_end, rot_start, rot_end,
                    gate_start, q_scale):
    j = pl.program_id(1)

    def run(act):
        w = w_ref[...].astype(BF16)
        for rows in _row_chunks(h_ref.shape[0], EPILOGUE_CHUNK_ROWS):
            y = _bdot(h_ref[rows, :], w)
            if act == "scale":
                y = y * q_scale
            elif act == "rotary":
                y = _rotate_half_pairs(y, cos_ref[rows, :], sin_ref[rows, :])
            elif act == "sigmoid":
                y = jax.nn.sigmoid(y)
            o_ref[rows, :] = y.astype(o_ref.dtype)

    rotary = (j >= rot_start) & (j < rot_end)
    pl.when(j < q_end)(lambda: run("scale"))
    pl.when(rotary)(lambda: run("rotary"))
    pl.when(j >= gate_start)(lambda: run("sigmoid"))
    pl.when((j >= q_end) & (j < gate_start) & jnp.logical_not(rotary))(lambda: run("none"))


def _in_proj(h, w, layer, seq, tm, tn, rope, width, q_scale):
    m, d = h.shape
    ncols = w.shape[2]
    spt = seq // tm
    return pl.pallas_call(
        functools.partial(_in_proj_kernel, q_end=width // tn, rot_start=3 * width // tn,
                          rot_end=5 * width // tn, gate_start=7 * width // tn, q_scale=q_scale),
        grid=(m // tm, ncols // tn),
        in_specs=[
            _panel_spec(tm, d),
            pl.BlockSpec((None, d, tn), lambda i, j: (layer, 0, j)),
            pl.BlockSpec((tm, HEAD_DIM), lambda i, j: (i % spt, 0)),
            pl.BlockSpec((tm, HEAD_DIM), lambda i, j: (i % spt, 0)),
        ],
        out_specs=pl.BlockSpec((tm, tn), lambda i, j: (i, j)),
        out_shape=jax.ShapeDtypeStruct((m, ncols), BF16),
        compiler_params=_params("arbitrary", "arbitrary"),
        name="in_proj",
    )(h, w, *rope)


def _branch_kernel(a_ref, b_ref, wa_ref, wb_ref, ga_ref, gb_ref, o_ref):
    wa = wa_ref[...].astype(BF16)
    wb = wb_ref[...].astype(BF16)
    for rows in _row_chunks(a_ref.shape[0], EPILOGUE_CHUNK_ROWS):
        oa = _bdot(a_ref[rows, :], wa)
        ob = _bdot(b_ref[rows, :], wb)
        o_ref[rows, :] = (ga_ref[rows, :].astype(F32) * oa
                          + gb_ref[rows, :].astype(F32) * ob).astype(o_ref.dtype)


def _branch_merge(attn, retg, w_a, w_b, layer, proj, gate_col0, tm, tn):
    m, ka = attn.shape
    kb = retg.shape[1]
    n = w_a.shape[2]
    tn = min(tn, n)
    nb = n // tn
    ja, jb = gate_col0 // tn, (gate_col0 + n) // tn
    return pl.pallas_call(
        _branch_kernel,
        grid=(m // tm, nb),
        in_specs=[
            _panel_spec(tm, ka),
            _panel_spec(tm, kb),
            pl.BlockSpec((None, ka, tn), lambda i, j: (layer, 0, j)),
            pl.BlockSpec((None, kb, tn), lambda i, j: (layer, 0, j)),
            pl.BlockSpec((tm, tn), lambda i, j: (i, j + ja)),
            pl.BlockSpec((tm, tn), lambda i, j: (i, j + jb)),
        ],
        out_specs=pl.BlockSpec((tm, tn), lambda i, j: (i, j)),
        out_shape=jax.ShapeDtypeStruct((m, n), BF16),
        compiler_params=_params("arbitrary", "arbitrary"),
        name="branch_merge",
    )(attn, retg, w_a, w_b, proj, proj)


PAIR = 2


def _sb_kernel(q_ref, k_ref, v_ref, o_ref, rhs_ref, mask_ref, *, ts, n_sub, n_first):
    @pl.when((pl.program_id(0) == 0) & (pl.program_id(1) == 0) & (pl.program_id(2) == 0))
    def _():
        row = lax.broadcasted_iota(jnp.int32, (ts, ts), 0)
        col = lax.broadcasted_iota(jnp.int32, (ts, ts), 1)
        suffix = jnp.concatenate([(row > col).astype(BF16), jnp.ones((ts, LANES), BF16)], axis=1)
        rhs_ref[...] = jnp.concatenate([suffix, suffix], axis=0)
        mask_ref[...] = (col < row).astype(F32)

    def weights_times_v(specs):
        starts = [pl.multiple_of(kb * ts, ts) for _, kb, _ in specs]
        zs = [lax.dot_general(q, k_ref[pl.ds(start, ts), :], (((1,), (1,)), ((), ())),
                              preferred_element_type=F32)
              for (q, _, _), start in zip(specs, starts)]
        log_betas, splits = [], []
        for (_, _, diagonal), z in zip(specs, zs):
            log_beta = jnp.minimum(z, 0.0) - jnp.log2(1.0 + jnp.exp2(-jnp.abs(z)))
            log_keep = log_beta - z
            if diagonal:
                log_keep = log_keep * mask_ref[...]
            hi = log_keep.astype(BF16)
            lo = (log_keep - hi.astype(F32)).astype(BF16)
            log_betas.append(log_beta)
            splits.append(jnp.concatenate([hi, lo], axis=1))
        sums = [_bdot(split, rhs_ref[...]) for split in splits]
        out = []
        for (_, _, diagonal), start, log_beta, s in zip(specs, starts, log_betas, sums):
            w = jnp.exp2(log_beta + s[:, :ts])
            if diagonal:
                w = w * mask_ref[...]
            out.append((_bdot(w.astype(BF16), v_ref[pl.ds(start, ts), :]), s[:, ts:]))
        return out

    def tile_specs(q, kb, count, diagonal):
        return [(q, jnp.maximum(kb - t, 0), diagonal and t == 0) for t in range(count)]

    def fold(results, kb, diagonal, carry, scale, acc):
        for t, (pv, total) in enumerate(results):
            if t > 0 or not diagonal:
                valid = jnp.where(kb - t >= 0, 1.0, 0.0)
                scale, total = scale * valid, total * valid
            acc = acc + scale * pv
            carry = carry + total
            scale = jnp.exp2(carry)
        return carry, scale, acc

    qs = [pl.program_id(2) * n_sub + sub for sub in range(n_sub)]
    q = [q_ref[sub * ts:(sub + 1) * ts, :] for sub in range(n_sub)]

    def advance(kbs, count, diagonal, states):
        results = weights_times_v([spec for sub in range(n_sub)
                                   for spec in tile_specs(q[sub], kbs[sub], count, diagonal)])
        live, new_states = 0.0, []
        for sub in range(n_sub):
            state = fold(results[sub * count:(sub + 1) * count], kbs[sub], diagonal, *states[sub])
            live = jnp.maximum(live, jnp.where(kbs[sub] - count >= 0, jnp.max(state[1]), 0.0))
            new_states.append(state)
        return live, tuple(new_states)

    zeros = jnp.zeros((ts, HEAD_DIM), F32)
    live, states = advance(qs, n_first, True, [(zeros, jnp.ones((ts, HEAD_DIM), F32), zeros)] * n_sub)

    def body(loop_state):
        it, _, states = loop_state
        kbs = [qs[sub] - n_first - PAIR * it for sub in range(n_sub)]
        return (it + 1,) + advance(kbs, PAIR, False, states)

    states = lax.while_loop(lambda loop_state: loop_state[1] > 0.0, body, (0, live, states))[2]
    for sub in range(n_sub):
        o_ref[sub * ts:(sub + 1) * ts, :] = states[sub][2].astype(o_ref.dtype)


def _stick_breaking(proj, batch, seq, tq, ts):
    m = proj.shape[0]
    nq = seq // tq
    return pl.pallas_call(
        functools.partial(_sb_kernel, ts=ts, n_sub=tq // ts, n_first=3),
        grid=(batch, N_HEADS, nq),
        in_specs=[
            pl.BlockSpec((tq, HEAD_DIM), lambda b, h, i: (b * nq + i, h)),
            pl.BlockSpec((seq, HEAD_DIM), lambda b, h, i: (b, N_HEADS + h)),
            pl.BlockSpec((seq, HEAD_DIM), lambda b, h, i: (b, 2 * N_HEADS + h)),
        ],
        out_specs=pl.BlockSpec((tq, HEAD_DIM), lambda b, h, i: (b * nq + i, h)),
        out_shape=jax.ShapeDtypeStruct((m, N_HEADS * HEAD_DIM), BF16),
        scratch_shapes=[pltpu.VMEM((2 * ts, ts + LANES), BF16), pltpu.VMEM((ts, ts), F32)],
        compiler_params=_params("arbitrary", "arbitrary", "arbitrary"),
        name="stick_breaking",
    )(proj, proj, proj)


def _ret_kernel(lg_ref, q_ref, k_ref, v_ref, gate_ref, gn_ref, o_ref, state_ref, entering_ref,
                *, chunk, n_chunks, scale):
    @pl.when(pl.program_id(2) == 0)
    def _():
        state_ref[...] = jnp.zeros_like(state_ref)

    lg = lg_ref[pl.program_id(1)]
    c = chunk
    row = lax.broadcasted_iota(jnp.int32, (c, c), 0).astype(F32)
    col = lax.broadcasted_iota(jnp.int32, (c, c), 1).astype(F32)
    diff = row - col
    decay = jnp.where(diff >= 0, jnp.exp(lg * jnp.maximum(diff, 0.0)), 0.0) * scale
    pos = lax.broadcasted_iota(jnp.int32, (c, HEAD_DIM), 0).astype(F32)
    q_decay = jnp.exp(lg * (pos + 1.0))
    k_decay = jnp.exp(lg * (c - 1.0 - pos)) * scale
    chunk_decay = jnp.exp(jnp.zeros((HEAD_DIM, HEAD_DIM), F32) + lg * c)
    gn = gn_ref[...]

    state = state_ref[...]
    for n in range(n_chunks):
        rows = slice(n * c, (n + 1) * c)
        entering_ref[n] = state.astype(BF16)
        kv = lax.dot_general((k_ref[rows, :].astype(F32) * k_decay).astype(BF16), v_ref[rows, :],
                             (((0,), (0,)), ((), ())), preferred_element_type=F32)
        state = chunk_decay * state + kv
    state_ref[...] = state

    for n in range(n_chunks):
        rows = slice(n * c, (n + 1) * c)
        q = q_ref[rows, :]
        scores = lax.dot_general(q, k_ref[rows, :], (((1,), (1,)), ((), ())),
                                 preferred_element_type=F32) * decay
        out = _bdot(scores.astype(BF16), v_ref[rows, :]) + q_decay * _bdot(q, entering_ref[n])
        mu = jnp.mean(out, axis=-1, keepdims=True)
        cen = out - mu
        var = jnp.mean(cen * cen, axis=-1, keepdims=True)
        normed = cen * lax.rsqrt(var + EPS) * gn
        g = gate_ref[rows, :].astype(F32)
        o_ref[rows, :] = (g * jax.nn.sigmoid(g) * normed).astype(o_ref.dtype)


def _retention(proj, head0, ret_gn, layer, batch, seq, rows_per_step):
    m = proj.shape[0]
    ns = seq // rows_per_step
    log_gamma = jnp.log1p(-jnp.exp2(-5.0 - jnp.arange(N_HEADS, dtype=F32)))
    blk = (rows_per_step, HEAD_DIM)

    def part(p):
        return pl.BlockSpec(blk, lambda b, h, s: (b * ns + s, head0 + p * N_HEADS + h))

    return pl.pallas_call(
        functools.partial(_ret_kernel, chunk=RET_CHUNK, n_chunks=rows_per_step // RET_CHUNK,
                          scale=HEAD_DIM ** -0.5),
        grid=(batch, N_HEADS, ns),
        in_specs=[
            pl.BlockSpec(memory_space=pltpu.SMEM),
            part(0), part(1), part(2), part(3),
            pl.BlockSpec((None, 1, HEAD_DIM), lambda b, h, s: (layer * N_HEADS + h, 0, 0)),
        ],
        out_specs=pl.BlockSpec(blk, lambda b, h, s: (b * ns + s, h)),
        out_shape=jax.ShapeDtypeStruct((m, N_HEADS * HEAD_DIM), BF16),
        scratch_shapes=[pltpu.VMEM((HEAD_DIM, HEAD_DIM), F32),
                        pltpu.VMEM((rows_per_step // RET_CHUNK, HEAD_DIM, HEAD_DIM), BF16)],
        compiler_params=_params("arbitrary", "arbitrary", "arbitrary"),
        name="retention",
    )(log_gamma, proj, proj, proj, proj, ret_gn.reshape(-1, 1, HEAD_DIM))


def _rope_tables(seq):
    half = HEAD_DIM // 2
    inv_freq = ROPE_BASE ** (-jnp.arange(half, dtype=F32) / half)
    ang = jnp.arange(seq, dtype=jnp.int32).astype(F32)[:, None] * inv_freq[None, :]
    cos, sin = jnp.cos(ang), jnp.sin(ang)
    return (jnp.concatenate([cos, cos], axis=-1), jnp.concatenate([-sin, sin], axis=-1))


def kernel(x, c, ada_down, ada_up, ada_bias, norm_ffn1, ffn1_in, ffn1_out, norm_mix, w_in, ret_gn,
           w_branch_a, w_branch_b, w_out, norm_ffn2, ffn2_in, ffn2_out, norm_final):
    batch, seq, d = x.shape
    depth = ada_down.shape[0]
    width = N_HEADS * HEAD_DIM
    tm_wide = min(2048, seq)
    tm_deep = min(512, seq)
    tq = min(1024, seq)
    tn = 512
    rope = _rope_tables(seq)
    table = _ada_modulation(c, ada_down, ada_up, ada_bias)
    xf = x.reshape(batch * seq, d)
    for l in range(depth):
        mod = _Mod(table, l, batch, seq)
        h = _rms_norm(xf, norm_ffn1, l, seq, mod, 0, 1)
        act, w_bf16 = _ffn_in(h, ffn1_in, ffn1_out, l, tm_wide)
        xf = _resid_matmul(act, w_bf16, 0, xf, mod, 2, 0.5, tm_deep, tn, resident_panel=False)
        h = _rms_norm(xf, norm_mix, l, seq, mod, 3, 4)
        proj = _in_proj(h, w_in, l, seq, tm_wide, tn, rope, width, HEAD_DIM ** -0.5 * LOG2_E)
        attn = _stick_breaking(proj, batch, seq, tq, HEAD_DIM)
        retg = _retention(proj, 3 * N_HEADS, ret_gn, l, batch, seq, min(2048, seq))
        merged = _branch_merge(attn, retg, w_branch_a, w_branch_b, l, proj, 7 * width, tm_wide, tn)
        xf = _resid_matmul(merged, w_out, l, xf, mod, 5, 1.0, tm_wide, tn)
        h = _rms_norm(xf, norm_ffn2, l, seq, mod, 6, 7)
        act, w_bf16 = _ffn_in(h, ffn2_in, ffn2_out, l, tm_wide)
        xf = _resid_matmul(act, w_bf16, 0, xf, mod, 8, 0.5, tm_deep, tn, resident_panel=False)
    out = _rms_norm(xf, norm_final.reshape(1, d), 0, seq, out_dtype=x.dtype)
    return out.reshape(batch, seq, d)
```

```python
import functools
import math

import jax
import jax.numpy as jnp
from jax import lax
from jax.experimental import pallas as pl
from jax.experimental.pallas import tpu as pltpu

BF16 = jnp.bfloat16
F32 = jnp.float32

LANES = 128
MXU_DIM = 256
VMEM_LIMIT_BYTES = 56 * 1024 * 1024

HEAD_DIM = 128
N_HEADS = 16
N_MOD = 9
RET_CHUNK = 128
ROPE_BASE = 10000.0
EPS = 1e-6
LOG2_E = math.log2(math.e)


def _params(*semantics):
    return pltpu.CompilerParams(dimension_semantics=semantics,
                                vmem_limit_bytes=VMEM_LIMIT_BYTES)


def _bdot(a, b):
    return jnp.dot(a, b, preferred_element_type=F32)


def _panel_spec(tm, k):
    return pl.BlockSpec((tm, k), lambda i, j: (i, 0), pipeline_mode=pl.Buffered(1))


def _ada_kernel(c_ref, down_ref, up_ref, bias_ref, o_ref, t_ref):
    @pl.when(pl.program_id(1) == 0)
    def _():
        c = c_ref[...]
        t_ref[...] = jnp.dot(c * jax.nn.sigmoid(c), down_ref[...], preferred_element_type=F32,
                             precision=lax.Precision.HIGHEST)

    o_ref[...] = jnp.dot(t_ref[...], up_ref[...], preferred_element_type=F32,
                         precision=lax.Precision.HIGHEST) + bias_ref[...]


def _ada_modulation(c, ada_down, ada_up, ada_bias):
    depth, d, r = ada_down.shape
    n = ada_up.shape[2]
    b = c.shape[0]
    rows = 8
    c_pad = jnp.zeros((rows, d), F32).at[:b].set(c)
    tn = min(n, 4096)
    out = pl.pallas_call(
        _ada_kernel,
        grid=(depth, n // tn),
        in_specs=[
            pl.BlockSpec((rows, d), lambda l, j: (0, 0)),
            pl.BlockSpec((None, d, r), lambda l, j: (l, 0, 0)),
            pl.BlockSpec((None, r, tn), lambda l, j: (l, 0, j)),
            pl.BlockSpec((None, 1, tn), lambda l, j: (l, 0, j)),
        ],
        out_specs=pl.BlockSpec((None, rows, tn), lambda l, j: (l, 0, j)),
        out_shape=jax.ShapeDtypeStruct((depth, rows, n), F32),
        scratch_shapes=[pltpu.VMEM((rows, r), F32)],
        compiler_params=_params("arbitrary", "arbitrary"),
        name="ada_modulation",
    )(c_pad, ada_down, ada_up, ada_bias.reshape(depth, 1, n))
    return out[:, :b].reshape(depth * b * N_MOD, 1, d)


class _Mod:
    def __init__(self, table, layer, batch, seq):
        self.table, self.base, self.seq = table, layer * batch * N_MOD, seq

    def row(self, first_token, idx):
        return self.base + (first_token // self.seq) * N_MOD + idx


NORM_ROWS = 16
NORM_UNROLL = 4


def _norm_kernel(x_ref, g_ref, *rest, modulated):
    o_ref, r_ref = rest[-2], rest[-1]
    gain = g_ref[...]
    if modulated:
        sh_ref, sc_ref = rest[0], rest[1]
        gain = gain * (1.0 + sc_ref[...])
    trips = x_ref.shape[0] // NORM_ROWS

    def group(r):
        return pl.ds(pl.multiple_of(r * NORM_ROWS, NORM_ROWS), NORM_ROWS)

    def factors(r, carry):
        x = x_ref[group(r), :]
        r_ref[group(r), :] = jnp.zeros((NORM_ROWS, LANES), F32) + lax.rsqrt(
            jnp.mean(x * x, axis=-1, keepdims=True) + EPS)
        return carry

    def scale(r, carry):
        factor = r_ref[group(r), :]
        y = x_ref[group(r), :] * jnp.concatenate([factor] * (x_ref.shape[1] // LANES), axis=1) * gain
        if modulated:
            y = y + sh_ref[...]
        o_ref[group(r), :] = y.astype(o_ref.dtype)
        return carry

    lax.fori_loop(0, trips, factors, 0, unroll=NORM_UNROLL)
    lax.fori_loop(0, trips, scale, 0, unroll=NORM_UNROLL)


def _rms_norm(x, g, layer, seq, mod=None, shift_idx=0, scale_idx=0, out_dtype=None):
    m, d = x.shape
    tm = min(512, seq)
    in_specs = [pl.BlockSpec((tm, d), lambda i: (i, 0)),
                pl.BlockSpec((None, 1, d), lambda i: (layer, 0, 0))]
    args = [x, g.reshape(-1, 1, d)]
    if mod is not None:
        def mod_spec(idx):
            return pl.BlockSpec((None, 1, d), lambda i: (mod.row(i * tm, idx), 0, 0))
        in_specs += [mod_spec(shift_idx), mod_spec(scale_idx)]
        args += [mod.table, mod.table]
    return pl.pallas_call(
        functools.partial(_norm_kernel, modulated=mod is not None),
        grid=(m // tm,),
        in_specs=in_specs,
        out_specs=pl.BlockSpec((tm, d), lambda i: (i, 0)),
        out_shape=jax.ShapeDtypeStruct((m, d), out_dtype or BF16),
        scratch_shapes=[pltpu.VMEM((tm, LANES), F32)],
        compiler_params=_params("arbitrary"),
        name="rms_norm",
    )(*args)


EPILOGUE_CHUNK_ROWS = 512


def _row_chunks(tm, rows):
    rows = min(rows, tm)
    return [slice(r, r + rows) for r in range(0, tm, rows)]


def _ffn_in_kernel(h_ref, wa_ref, wb_ref, wout_ref, o_ref, wout_bf16_ref):
    wout_bf16_ref[...] = wout_ref[...].astype(BF16)
    wa = wa_ref[...].astype(BF16)
    wb = wb_ref[...].astype(BF16)
    for rows in _row_chunks(h_ref.shape[0], EPILOGUE_CHUNK_ROWS):
        h = h_ref[rows, :]
        a = _bdot(h, wa)
        b = _bdot(h, wb)
        o_ref[rows, :] = (a * jax.nn.sigmoid(a) * b).astype(o_ref.dtype)


def _ffn_in(h, w_in, w_out, layer, tm):
    m, d = h.shape
    f = w_in.shape[2] // 2
    n_out = w_out.shape[2]
    tn = MXU_DIM
    nb = f // tn
    cast_rows = f // (m // tm * nb)
    assert cast_rows * (m // tm * nb) == f and cast_rows % 16 == 0, (f, m // tm, nb)
    return pl.pallas_call(
        _ffn_in_kernel,
        grid=(m // tm, nb),
        in_specs=[
            _panel_spec(tm, d),
            pl.BlockSpec((None, d, tn), lambda i, j: (layer, 0, j)),
            pl.BlockSpec((None, d, tn), lambda i, j: (layer, 0, j + nb)),
            pl.BlockSpec((None, cast_rows, n_out), lambda i, j: (layer, i * nb + j, 0)),
        ],
        out_specs=[pl.BlockSpec((tm, tn), lambda i, j: (i, j)),
                   pl.BlockSpec((None, cast_rows, n_out), lambda i, j: (0, i * nb + j, 0))],
        out_shape=[jax.ShapeDtypeStruct((m, f), BF16),
                   jax.ShapeDtypeStruct((1, f, n_out), BF16)],
        compiler_params=_params("arbitrary", "arbitrary"),
        name="ffn_in",
    )(h, w_in, w_in, w_out)


def _resid_kernel(a_ref, w_ref, x_ref, g_ref, o_ref, *, coef):
    w = w_ref[...].astype(BF16)
    gate = coef * g_ref[...]
    for rows in _row_chunks(a_ref.shape[0], EPILOGUE_CHUNK_ROWS):
        o_ref[rows, :] = x_ref[rows, :] + gate * _bdot(a_ref[rows, :], w)


def _resid_matmul(a, w, layer, x, mod, gate_idx, coef, tm, tn, resident_panel=True):
    m, k = a.shape
    n = w.shape[2]
    tn = min(tn, n)
    return pl.pallas_call(
        functools.partial(_resid_kernel, coef=coef),
        grid=(m // tm, n // tn),
        in_specs=[
            _panel_spec(tm, k) if resident_panel else pl.BlockSpec((tm, k), lambda i, j: (i, 0)),
            pl.BlockSpec((None, k, tn), lambda i, j: (layer, 0, j)),
            pl.BlockSpec((tm, tn), lambda i, j: (i, j)),
            pl.BlockSpec((None, 1, tn), lambda i, j: (mod.row(i * tm, gate_idx), 0, j)),
        ],
        out_specs=pl.BlockSpec((tm, tn), lambda i, j: (i, j)),
        out_shape=jax.ShapeDtypeStruct((m, n), F32),
        compiler_params=_params("arbitrary", "arbitrary"),
        name="resid_matmul",
    )(a, w, x, mod.table)


def _rotate_half_pairs(y, cos, sin_signed):
    parts = []
    for s in range(y.shape[1] // HEAD_DIM):
        blk = y[:, s * HEAD_DIM:(s + 1) * HEAD_DIM]
        parts.append(blk * cos + pltpu.roll(blk, HEAD_DIM // 2, 1) * sin_signed)
    return jnp.concatenate(parts, axis=1)


def _in_proj_kernel(h_ref, w_ref, cos_ref, sin_ref, o_ref, *, q_end, rot_start, rot_end,
                    gate_start, q_scale):
    j = pl.program_id(1)

    def run(act):
        w = w_ref[...].astype(BF16)
        for rows in _row_chunks(h_ref.shape[0], EPILOGUE_CHUNK_ROWS):
            y = _bdot(h_ref[rows, :], w)
            if act == "scale":
                y = y * q_scale
            elif act == "rotary":
                y = _rotate_half_pairs(y, cos_ref[rows, :], sin_ref[rows, :])
            elif act == "sigmoid":
                y = jax.nn.sigmoid(y)
            o_ref[rows, :] = y.astype(o_ref.dtype)

    rotary = (j >= rot_start) & (j < rot_end)
    pl.when(j < q_end)(lambda: run("scale"))
    pl.when(rotary)(lambda: run("rotary"))
    pl.when(j >= gate_start)(lambda: run("sigmoid"))
    pl.when((j >= q_end) & (j < gate_start) & jnp.logical_not(rotary))(lambda: run("none"))


def _in_proj(h, w, layer, seq, tm, tn, rope, width, q_scale):
    m, d = h.shape
    ncols = w.shape[2]
    spt = seq // tm
    return pl.pallas_call(
        functools.partial(_in_proj_kernel, q_end=width // tn, rot_start=3 * width // tn,
                          rot_end=5 * width // tn, gate_start=7 * width // tn, q_scale=q_scale),
        grid=(m // tm, ncols // tn),
        in_specs=[
            _panel_spec(tm, d),
            pl.BlockSpec((None, d, tn), lambda i, j: (layer, 0, j)),
            pl.BlockSpec((tm, HEAD_DIM), lambda i, j: (i % spt, 0)),
            pl.BlockSpec((tm, HEAD_DIM), lambda i, j: (i % spt, 0)),
        ],
        out_specs=pl.BlockSpec((tm, tn), lambda i, j: (i, j)),
        out_shape=jax.ShapeDtypeStruct((m, ncols), BF16),
        compiler_params=_params("arbitrary", "arbitrary"),
        name="in_proj",
    )(h, w, *rope)


def _branch_kernel(a_ref, b_ref, wa_ref, wb_ref, ga_ref, gb_ref, o_ref):
    wa = wa_ref[...].astype(BF16)
    wb = wb_ref[...].astype(BF16)
    for rows in _row_chunks(a_ref.shape[0], EPILOGUE_CHUNK_ROWS):
        oa = _bdot(a_ref[rows, :], wa)
        ob = _bdot(b_ref[rows, :], wb)
        o_ref[rows, :] = (ga_ref[rows, :].astype(F32) * oa
                          + gb_ref[rows, :].astype(F32) * ob).astype(o_ref.dtype)


def _branch_merge(attn, retg, w_a, w_b, layer, proj, gate_col0, tm, tn):
    m, ka = attn.shape
    kb = retg.shape[1]
    n = w_a.shape[2]
    tn = min(tn, n)
    nb = n // tn
    ja, jb = gate_col0 // tn, (gate_col0 + n) // tn
    return pl.pallas_call(
        _branch_kernel,
        grid=(m // tm, nb),
        in_specs=[
            _panel_spec(tm, ka),
            _panel_spec(tm, kb),
            pl.BlockSpec((None, ka, tn), lambda i, j: (layer, 0, j)),
            pl.BlockSpec((None, kb, tn), lambda i, j: (layer, 0, j)),
            pl.BlockSpec((tm, tn), lambda i, j: (i, j + ja)),
            pl.BlockSpec((tm, tn), lambda i, j: (i, j + jb)),
        ],
        out_specs=pl.BlockSpec((tm, tn), lambda i, j: (i, j)),
        out_shape=jax.ShapeDtypeStruct((m, n), BF16),
        compiler_params=_params("arbitrary", "arbitrary"),
        name="branch_merge",
    )(attn, retg, w_a, w_b, proj, proj)


PAIR = 2


def _sb_kernel(q_ref, k_ref, v_ref, o_ref, rhs_ref, mask_ref, *, ts, n_sub, n_first):
    @pl.when((pl.program_id(0) == 0) & (pl.program_id(1) == 0) & (pl.program_id(2) == 0))
    def _():
        row = lax.broadcasted_iota(jnp.int32, (ts, ts), 0)
        col = lax.broadcasted_iota(jnp.int32, (ts, ts), 1)
        suffix = jnp.concatenate([(row > col).astype(BF16), jnp.ones((ts, LANES), BF16)], axis=1)
        rhs_ref[...] = jnp.concatenate([suffix, suffix], axis=0)
        mask_ref[...] = (col < row).astype(F32)

    def weights_times_v(specs):
        starts = [pl.multiple_of(kb * ts, ts) for _, kb, _ in specs]
        zs = [lax.dot_general(q, k_ref[pl.ds(start, ts), :], (((1,), (1,)), ((), ())),
                              preferred_element_type=F32)
              for (q, _, _), start in zip(specs, starts)]
        log_betas, splits = [], []
        for (_, _, diagonal), z in zip(specs, zs):
            log_beta = jnp.minimum(z, 0.0) - jnp.log2(1.0 + jnp.exp2(-jnp.abs(z)))
            log_keep = log_beta - z
            if diagonal:
                log_keep = log_keep * mask_ref[...]
            hi = log_keep.astype(BF16)
            lo = (log_keep - hi.astype(F32)).astype(BF16)
            log_betas.append(log_beta)
            splits.append(jnp.concatenate([hi, lo], axis=1))
        sums = [_bdot(split, rhs_ref[...]) for split in splits]
        out = []
        for (_, _, diagonal), start, log_beta, s in zip(specs, starts, log_betas, sums):
            w = jnp.exp2(log_beta + s[:, :ts])
            if diagonal:
                w = w * mask_ref[...]
            out.append((_bdot(w.astype(BF16), v_ref[pl.ds(start, ts), :]), s[:, ts:]))
        return out

    def tile_specs(q, kb, count, diagonal):
        return [(q, jnp.maximum(kb - t, 0), diagonal and t == 0) for t in range(count)]

    def fold(results, kb, diagonal, carry, scale, acc):
        for t, (pv, total) in enumerate(results):
            if t > 0 or not diagonal:
                valid = jnp.where(kb - t >= 0, 1.0, 0.0)
                scale, total = scale * valid, total * valid
            acc = acc + scale * pv
            carry = carry + total
            scale = jnp.exp2(carry)
        return carry, scale, acc

    qs = [pl.program_id(2) * n_sub + sub for sub in range(n_sub)]
    q = [q_ref[sub * ts:(sub + 1) * ts, :] for sub in range(n_sub)]

    def advance(kbs, count, diagonal, states):
        results = weights_times_v([spec for sub in range(n_sub)
                                   for spec in tile_specs(q[sub], kbs[sub], count, diagonal)])
        live, new_states = 0.0, []
        for sub in range(n_sub):
            state = fold(results[sub * count:(sub + 1) * count], kbs[sub], diagonal, *states[sub])
            live = jnp.maximum(live, jnp.where(kbs[sub] - count >= 0, jnp.max(state[1]), 0.0))
            new_states.append(state)
        return live, tuple(new_states)

    zeros = jnp.zeros((ts, HEAD_DIM), F32)
    live, states = advance(qs, n_first, True, [(zeros, jnp.ones((ts, HEAD_DIM), F32), zeros)] * n_sub)

    def body(loop_state):
        it, _, states = loop_state
        kbs = [qs[sub] - n_first - PAIR * it for sub in range(n_sub)]
        return (it + 1,) + advance(kbs, PAIR, False, states)

    states = lax.while_loop(lambda loop_state: loop_state[1] > 0.0, body, (0, live, states))[2]
    for sub in range(n_sub):
        o_ref[sub * ts:(sub + 1) * ts, :] = states[sub][2].astype(o_ref.dtype)


def _stick_breaking(proj, batch, seq, tq, ts):
    m = proj.shape[0]
    nq = seq // tq
    return pl.pallas_call(
        functools.partial(_sb_kernel, ts=ts, n_sub=tq // ts, n_first=3),
        grid=(batch, N_HEADS, nq),
        in_specs=[
            pl.BlockSpec((tq, HEAD_DIM), lambda b, h, i: (b * nq + i, h)),
            pl.BlockSpec((seq, HEAD_DIM), lambda b, h, i: (b, N_HEADS + h)),
            pl.BlockSpec((seq, HEAD_DIM), lambda b, h, i: (b, 2 * N_HEADS + h)),
        ],
        out_specs=pl.BlockSpec((tq, HEAD_DIM), lambda b, h, i: (b * nq + i, h)),
        out_shape=jax.ShapeDtypeStruct((m, N_HEADS * HEAD_DIM), BF16),
        scratch_shapes=[pltpu.VMEM((2 * ts, ts + LANES), BF16), pltpu.VMEM((ts, ts), F32)],
        compiler_params=_params("arbitrary", "arbitrary", "arbitrary"),
        name="stick_breaking",
    )(proj, proj, proj)


def _ret_kernel(lg_ref, q_ref, k_ref, v_ref, gate_ref, gn_ref, o_ref, state_ref, entering_ref,
                *, chunk, n_chunks, scale):
    @pl.when(pl.program_id(2) == 0)
    def _():
        state_ref[...] = jnp.zeros_like(state_ref)

    lg = lg_ref[pl.program_id(1)]
    c = chunk
    row = lax.broadcasted_iota(jnp.int32, (c, c), 0).astype(F32)
    col = lax.broadcasted_iota(jnp.int32, (c, c), 1).astype(F32)
    diff = row - col
    decay = jnp.where(diff >= 0, jnp.exp(lg * jnp.maximum(diff, 0.0)), 0.0) * scale
    pos = lax.broadcasted_iota(jnp.int32, (c, HEAD_DIM), 0).astype(F32)
    q_decay = jnp.exp(lg * (pos + 1.0))
    k_decay = jnp.exp(lg * (c - 1.0 - pos)) * scale
    chunk_decay = jnp.exp(jnp.zeros((HEAD_DIM, HEAD_DIM), F32) + lg * c)
    gn = gn_ref[...]

    state = state_ref[...]
    for n in range(n_chunks):
        rows = slice(n * c, (n + 1) * c)
        entering_ref[n] = state.astype(BF16)
        kv = lax.dot_general((k_ref[rows, :].astype(F32) * k_decay).astype(BF16), v_ref[rows, :],
                             (((0,), (0,)), ((), ())), preferred_element_type=F32)
        state = chunk_decay * state + kv
    state_ref[...] = state

    for n in range(n_chunks):
        rows = slice(n * c, (n + 1) * c)
        q = q_ref[rows, :]
        scores = lax.dot_general(q, k_ref[rows, :], (((1,), (1,)), ((), ())),
                                 preferred_element_type=F32) * decay
        out = _bdot(scores.astype(BF16), v_ref[rows, :]) + q_decay * _bdot(q, entering_ref[n])
        mu = jnp.mean(out, axis=-1, keepdims=True)
        cen = out - mu
        var = jnp.mean(cen * cen, axis=-1, keepdims=True)
        normed = cen * lax.rsqrt(var + EPS) * gn
        g = gate_ref[rows, :].astype(F32)
        o_ref[rows, :] = (g * jax.nn.sigmoid(g) * normed).astype(o_ref.dtype)


def _retention(proj, head0, ret_gn, layer, batch, seq, rows_per_step):
    m = proj.shape[0]
    ns = seq // rows_per_step
    log_gamma = jnp.log1p(-jnp.exp2(-5.0 - jnp.arange(N_HEADS, dtype=F32)))
    blk = (rows_per_step, HEAD_DIM)

    def part(p):
        return pl.BlockSpec(blk, lambda b, h, s: (b * ns + s, head0 + p * N_HEADS + h))

    return pl.pallas_call(
        functools.partial(_ret_kernel, chunk=RET_CHUNK, n_chunks=rows_per_step // RET_CHUNK,
                          scale=HEAD_DIM ** -0.5),
        grid=(batch, N_HEADS, ns),
        in_specs=[
            pl.BlockSpec(memory_space=pltpu.SMEM),
            part(0), part(1), part(2), part(3),
            pl.BlockSpec((None, 1, HEAD_DIM), lambda b, h, s: (layer * N_HEADS + h, 0, 0)),
        ],
        out_specs=pl.BlockSpec(blk, lambda b, h, s: (b * ns + s, h)),
        out_shape=jax.ShapeDtypeStruct((m, N_HEADS * HEAD_DIM), BF16),
        scratch_shapes=[pltpu.VMEM((HEAD_DIM, HEAD_DIM), F32),
                        pltpu.VMEM((rows_per_step // RET_CHUNK, HEAD_DIM, HEAD_DIM), BF16)],
        compiler_params=_params("arbitrary", "arbitrary", "arbitrary"),
        name="retention",
    )(log_gamma, proj, proj, proj, proj, ret_gn.reshape(-1, 1, HEAD_DIM))


def _rope_tables(seq):
    half = HEAD_DIM // 2
    inv_freq = ROPE_BASE ** (-jnp.arange(half, dtype=F32) / half)
    ang = jnp.arange(seq, dtype=jnp.int32).astype(F32)[:, None] * inv_freq[None, :]
    cos, sin = jnp.cos(ang), jnp.sin(ang)
    return (jnp.concatenate([cos, cos], axis=-1), jnp.concatenate([-sin, sin], axis=-1))


def kernel(x, c, ada_down, ada_up, ada_bias, norm_ffn1, ffn1_in, ffn1_out, norm_mix, w_in, ret_gn,
           w_branch_a, w_branch_b, w_out, norm_ffn2, ffn2_in, ffn2_out, norm_final):
    batch, seq, d = x.shape
    depth = ada_down.shape[0]
    width = N_HEADS * HEAD_DIM
    tm_wide = min(2048, seq)
    tm_deep = min(512, seq)
    tq = min(1024, seq)
    tn = 512
    rope = _rope_tables(seq)
    table = _ada_modulation(c, ada_down, ada_up, ada_bias)
    xf = x.reshape(batch * seq, d)
    for l in range(depth):
        mod = _Mod(table, l, batch, seq)
        h = _rms_norm(xf, norm_ffn1, l, seq, mod, 0, 1)
        act, w_bf16 = _ffn_in(h, ffn1_in, ffn1_out, l, tm_wide)
        xf = _resid_matmul(act, w_bf16, 0, xf, mod, 2, 0.5, tm_deep, tn, resident_panel=False)
        h = _rms_norm(xf, norm_mix, l, seq, mod, 3, 4)
        proj = _in_proj(h, w_in, l, seq, tm_wide, tn, rope, width, HEAD_DIM ** -0.5 * LOG2_E)
        attn = _stick_breaking(proj, batch, seq, tq, HEAD_DIM)
        retg = _retention(proj, 3 * N_HEADS, ret_gn, l, batch, seq, min(2048, seq))
        merged = _branch_merge(attn, retg, w_branch_a, w_branch_b, l, proj, 7 * width, tm_wide, tn)
        xf = _resid_matmul(merged, w_out, l, xf, mod, 5, 1.0, tm_wide, tn)
        h = _rms_norm(xf, norm_ffn2, l, seq, mod, 6, 7)
        act, w_bf16 = _ffn_in(h, ffn2_in, ffn2_out, l, tm_wide)
        xf = _resid_matmul(act, w_bf16, 0, xf, mod, 8, 0.5, tm_deep, tn, resident_panel=False)
    out = _rms_norm(xf, norm_final.reshape(1, d), 0, seq, out_dtype=x.dtype)
    return out.reshape(batch, seq, d)
```

```python
import functools
import math

import jax
import jax.numpy as jnp
from jax import lax
from jax.experimental import pallas as pl
from jax.experimental.pallas import tpu as pltpu

BF16 = jnp.bfloat16
F32 = jnp.float32

LANES = 128
MXU_DIM = 256
VMEM_LIMIT_BYTES = 56 * 1024 * 1024

HEAD_DIM = 128
N_HEADS = 16
N_MOD = 9
RET_CHUNK = 128
ROPE_BASE = 10000.0
EPS = 1e-6
LOG2_E = math.log2(math.e)


def _params(*semantics):
    return pltpu.CompilerParams(dimension_semantics=semantics,
                                vmem_limit_bytes=VMEM_LIMIT_BYTES)


def _bdot(a, b):
    return jnp.dot(a, b, preferred_element_type=F32)


def _panel_spec(tm, k):
    return pl.BlockSpec((tm, k), lambda i, j: (i, 0), pipeline_mode=pl.Buffered(1))


def _ada_kernel(c_ref, down_ref, up_ref, bias_ref, o_ref, t_ref):
    @pl.when(pl.program_id(1) == 0)
    def _():
        c = c_ref[...]
        t_ref[...] = jnp.dot(c * jax.nn.sigmoid(c), down_ref[...], preferred_element_type=F32,
                             precision=lax.Precision.HIGHEST)

    o_ref[...] = jnp.dot(t_ref[...], up_ref[...], preferred_element_type=F32,
                         precision=lax.Precision.HIGHEST) + bias_ref[...]


def _ada_modulation(c, ada_down, ada_up, ada_bias):
    depth, d, r = ada_down.shape
    n = ada_up.shape[2]
    b = c.shape[0]
    rows = 8
    c_pad = jnp.zeros((rows, d), F32).at[:b].set(c)
    tn = min(n, 4096)
    out = pl.pallas_call(
        _ada_kernel,
        grid=(depth, n // tn),
        in_specs=[
            pl.BlockSpec((rows, d), lambda l, j: (0, 0)),
            pl.BlockSpec((None, d, r), lambda l, j: (l, 0, 0)),
            pl.BlockSpec((None, r, tn), lambda l, j: (l, 0, j)),
            pl.BlockSpec((None, 1, tn), lambda l, j: (l, 0, j)),
        ],
        out_specs=pl.BlockSpec((None, rows, tn), lambda l, j: (l, 0, j)),
        out_shape=jax.ShapeDtypeStruct((depth, rows, n), F32),
        scratch_shapes=[pltpu.VMEM((rows, r), F32)],
        compiler_params=_params("arbitrary", "arbitrary"),
        name="ada_modulation",
    )(c_pad, ada_down, ada_up, ada_bias.reshape(depth, 1, n))
    return out[:, :b].reshape(depth * b * N_MOD, 1, d)


class _Mod:
    def __init__(self, table, layer, batch, seq):
        self.table, self.base, self.seq = table, layer * batch * N_MOD, seq

    def row(self, first_token, idx):
        return self.base + (first_token // self.seq) * N_MOD + idx


NORM_ROWS = 16
NORM_UNROLL = 4


def _norm_kernel(x_ref, g_ref, *rest, modulated):
    o_ref, r_ref = rest[-2], rest[-1]
    gain = g_ref[...]
    if modulated:
        sh_ref, sc_ref = rest[0], rest[1]
        gain = gain * (1.0 + sc_ref[...])
    trips = x_ref.shape[0] // NORM_ROWS

    def group(r):
        return pl.ds(pl.multiple_of(r * NORM_ROWS, NORM_ROWS), NORM_ROWS)

    def factors(r, carry):
        x = x_ref[group(r), :]
        r_ref[group(r), :] = jnp.zeros((NORM_ROWS, LANES), F32) + lax.rsqrt(
            jnp.mean(x * x, axis=-1, keepdims=True) + EPS)
        return carry

    def scale(r, carry):
        factor = r_ref[group(r), :]
        y = x_ref[group(r), :] * jnp.concatenate([factor] * (x_ref.shape[1] // LANES), axis=1) * gain
        if modulated:
            y = y + sh_ref[...]
        o_ref[group(r), :] = y.astype(o_ref.dtype)
        return carry

    lax.fori_loop(0, trips, factors, 0, unroll=NORM_UNROLL)
    lax.fori_loop(0, trips, scale, 0, unroll=NORM_UNROLL)


def _rms_norm(x, g, layer, seq, mod=None, shift_idx=0, scale_idx=0, out_dtype=None):
    m, d = x.shape
    tm = min(512, seq)
    in_specs = [pl.BlockSpec((tm, d), lambda i: (i, 0)),
                pl.BlockSpec((None, 1, d), lambda i: (layer, 0, 0))]
    args = [x, g.reshape(-1, 1, d)]
    if mod is not None:
        def mod_spec(idx):
            return pl.BlockSpec((None, 1, d), lambda i: (mod.row(i * tm, idx), 0, 0))
        in_specs += [mod_spec(shift_idx), mod_spec(scale_idx)]
        args += [mod.table, mod.table]
    return pl.pallas_call(
        functools.partial(_norm_kernel, modulated=mod is not None),
        grid=(m // tm,),
        in_specs=in_specs,
        out_specs=pl.BlockSpec((tm, d), lambda i: (i, 0)),
        out_shape=jax.ShapeDtypeStruct((m, d), out_dtype or BF16),
        scratch_shapes=[pltpu.VMEM((tm, LANES), F32)],
        compiler_params=_params("arbitrary"),
        name="rms_norm",
    )(*args)


EPILOGUE_CHUNK_ROWS = 1024


def _row_chunks(tm, rows):
    rows = min(rows, tm)
    return [slice(r, r + rows) for r in range(0, tm, rows)]


def _ffn_in_kernel(h_ref, wa_ref, wb_ref, wout_ref, o_ref, wout_bf16_ref):
    wout_bf16_ref[...] = wout_ref[...].astype(BF16)
    wa = wa_ref[...].astype(BF16)
    wb = wb_ref[...].astype(BF16)
    for rows in _row_chunks(h_ref.shape[0], EPILOGUE_CHUNK_ROWS):
        h = h_ref[rows, :]
        a = _bdot(h, wa)
        b = _bdot(h, wb)
        o_ref[rows, :] = (a * jax.nn.sigmoid(a) * b).astype(o_ref.dtype)


def _ffn_in(h, w_in, w_out, layer, tm):
    m, d = h.shape
    f = w_in.shape[2] // 2
    n_out = w_out.shape[2]
    tn = MXU_DIM
    nb = f // tn
    cast_rows = f // (m // tm * nb)
    assert cast_rows * (m // tm * nb) == f and cast_rows % 16 == 0, (f, m // tm, nb)
    return pl.pallas_call(
        _ffn_in_kernel,
        grid=(m // tm, nb),
        in_specs=[
            _panel_spec(tm, d),
            pl.BlockSpec((None, d, tn), lambda i, j: (layer, 0, j)),
            pl.BlockSpec((None, d, tn), lambda i, j: (layer, 0, j + nb)),
            pl.BlockSpec((None, cast_rows, n_out), lambda i, j: (layer, i * nb + j, 0)),
        ],
        out_specs=[pl.BlockSpec((tm, tn), lambda i, j: (i, j)),
                   pl.BlockSpec((None, cast_rows, n_out), lambda i, j: (0, i * nb + j, 0))],
        out_shape=[jax.ShapeDtypeStruct((m, f), BF16),
                   jax.ShapeDtypeStruct((1, f, n_out), BF16)],
        compiler_params=_params("arbitrary", "arbitrary"),
        name="ffn_in",
    )(h, w_in, w_in, w_out)


def _resid_kernel(a_ref, w_ref, x_ref, g_ref, o_ref, *, coef):
    w = w_ref[...].astype(BF16)
    gate = coef * g_ref[...]
    for rows in _row_chunks(a_ref.shape[0], EPILOGUE_CHUNK_ROWS):
        o_ref[rows, :] = x_ref[rows, :] + gate * _bdot(a_ref[rows, :], w)


def _resid_matmul(a, w, layer, x, mod, gate_idx, coef, tm, tn, resident_panel=True):
    m, k = a.shape
    n = w.shape[2]
    tn = min(tn, n)
    return pl.pallas_call(
        functools.partial(_resid_kernel, coef=coef),
        grid=(m // tm, n // tn),
        in_specs=[
            _panel_spec(tm, k) if resident_panel else pl.BlockSpec((tm, k), lambda i, j: (i, 0)),
            pl.BlockSpec((None, k, tn), lambda i, j: (layer, 0, j)),
            pl.BlockSpec((tm, tn), lambda i, j: (i, j)),
            pl.BlockSpec((None, 1, tn), lambda i, j: (mod.row(i * tm, gate_idx), 0, j)),
        ],
        out_specs=pl.BlockSpec((tm, tn), lambda i, j: (i, j)),
        out_shape=jax.ShapeDtypeStruct((m, n), F32),
        compiler_params=_params("arbitrary", "arbitrary"),
        name="resid_matmul",
    )(a, w, x, mod.table)


def _rotate_half_pairs(y, cos, sin_signed):
    parts = []
    for s in range(y.shape[1] // HEAD_DIM):
        blk = y[:, s * HEAD_DIM:(s + 1) * HEAD_DIM]
        parts.append(blk * cos + pltpu.roll(blk, HEAD_DIM // 2, 1) * sin_signed)
    return jnp.concatenate(parts, axis=1)


def _in_proj_kernel(h_ref, w_ref, cos_ref, sin_ref, o_ref, *, q_end, rot_start, rot_end,
                    gate_start, q_scale):
    j = pl.program_id(1)

    def run(act):
        w = w_ref[...].astype(BF16)
        for rows in _row_chunks(h_ref.shape[0], EPILOGUE_CHUNK_ROWS):
            y = _bdot(h_ref[rows, :], w)
            if act == "scale":
                y = y * q_scale
            elif act == "rotary":
                y = _rotate_half_pairs(y, cos_ref[rows, :], sin_ref[rows, :])
            elif act == "sigmoid":
                y = jax.nn.sigmoid(y)
            o_ref[rows, :] = y.astype(o_ref.dtype)

    rotary = (j >= rot_start) & (j < rot_end)
    pl.when(j < q_end)(lambda: run("scale"))
    pl.when(rotary)(lambda: run("rotary"))
    pl.when(j >= gate_start)(lambda: run("sigmoid"))
    pl.when((j >= q_end) & (j < gate_start) & jnp.logical_not(rotary))(lambda: run("none"))


def _in_proj(h, w, layer, seq, tm, tn, rope, width, q_scale):
    m, d = h.shape
    ncols = w.shape[2]
    spt = seq // tm
    return pl.pallas_call(
        functools.partial(_in_proj_kernel, q_end=width // tn, rot_start=3 * width // tn,
                          rot_end=5 * width // tn, gate_start=7 * width // tn, q_scale=q_scale),
        grid=(m // tm, ncols // tn),
        in_specs=[
            _panel_spec(tm, d),
            pl.BlockSpec((None, d, tn), lambda i, j: (layer, 0, j)),
            pl.BlockSpec((tm, HEAD_DIM), lambda i, j: (i % spt, 0)),
            pl.BlockSpec((tm, HEAD_DIM), lambda i, j: (i % spt, 0)),
        ],
        out_specs=pl.BlockSpec((tm, tn), lambda i, j: (i, j)),
        out_shape=jax.ShapeDtypeStruct((m, ncols), BF16),
        compiler_params=_params("arbitrary", "arbitrary"),
        name="in_proj",
    )(h, w, *rope)


def _branch_kernel(a_ref, b_ref, wa_ref, wb_ref, ga_ref, gb_ref, o_ref):
    wa = wa_ref[...].astype(BF16)
    wb = wb_ref[...].astype(BF16)
    for rows in _row_chunks(a_ref.shape[0], EPILOGUE_CHUNK_ROWS):
        oa = _bdot(a_ref[rows, :], wa)
        ob = _bdot(b_ref[rows, :], wb)
        o_ref[rows, :] = (ga_ref[rows, :].astype(F32) * oa
                          + gb_ref[rows, :].astype(F32) * ob).astype(o_ref.dtype)


def _branch_merge(attn, retg, w_a, w_b, layer, proj, gate_col0, tm, tn):
    m, ka = attn.shape
    kb = retg.shape[1]
    n = w_a.shape[2]
    tn = min(tn, n)
    nb = n // tn
    ja, jb = gate_col0 // tn, (gate_col0 + n) // tn
    return pl.pallas_call(
        _branch_kernel,
        grid=(m // tm, nb),
        in_specs=[
            _panel_spec(tm, ka),
            _panel_spec(tm, kb),
            pl.BlockSpec((None, ka, tn), lambda i, j: (layer, 0, j)),
            pl.BlockSpec((None, kb, tn), lambda i, j: (layer, 0, j)),
            pl.BlockSpec((tm, tn), lambda i, j: (i, j + ja)),
            pl.BlockSpec((tm, tn), lambda i, j: (i, j + jb)),
        ],
        out_specs=pl.BlockSpec((tm, tn), lambda i, j: (i, j)),
        out_shape=jax.ShapeDtypeStruct((m, n), BF16),
        compiler_params=_params("arbitrary", "arbitrary"),
        name="branch_merge",
    )(attn, retg, w_a, w_b, proj, proj)


PAIR = 2


def _sb_kernel(q_ref, k_ref, v_ref, o_ref, rhs_ref, mask_ref, *, ts, n_sub, n_first):
    @pl.when((pl.program_id(0) == 0) & (pl.program_id(1) == 0) & (pl.program_id(2) == 0))
    def _():
        row = lax.broadcasted_iota(jnp.int32, (ts, ts), 0)
        col = lax.broadcasted_iota(jnp.int32, (ts, ts), 1)
        suffix = jnp.concatenate([(row > col).astype(BF16), jnp.ones((ts, LANES), BF16)], axis=1)
        rhs_ref[...] = jnp.concatenate([suffix, suffix], axis=0)
        mask_ref[...] = (col < row).astype(F32)

    def weights_times_v(specs):
        starts = [pl.multiple_of(kb * ts, ts) for _, kb, _ in specs]
        zs = [lax.dot_general(q, k_ref[pl.ds(start, ts), :], (((1,), (1,)), ((), ())),
                              preferred_element_type=F32)
              for (q, _, _), start in zip(specs, starts)]
        log_betas, splits = [], []
        for (_, _, diagonal), z in zip(specs, zs):
            log_beta = jnp.minimum(z, 0.0) - jnp.log2(1.0 + jnp.exp2(-jnp.abs(z)))
            log_keep = log_beta - z
            if diagonal:
                log_keep = log_keep * mask_ref[...]
            hi = log_keep.astype(BF16)
            lo = (log_keep - hi.astype(F32)).astype(BF16)
            log_betas.append(log_beta)
            splits.append(jnp.concatenate([hi, lo], axis=1))
        sums = [_bdot(split, rhs_ref[...]) for split in splits]
        out = []
        for (_, _, diagonal), start, log_beta, s in zip(specs, starts, log_betas, sums):
            w = jnp.exp2(log_beta + s[:, :ts])
            if diagonal:
                w = w * mask_ref[...]
            out.append((_bdot(w.astype(BF16), v_ref[pl.ds(start, ts), :]), s[:, ts:]))
        return out

    def tile_specs(q, kb, count, diagonal):
        return [(q, jnp.maximum(kb - t, 0), diagonal and t == 0) for t in range(count)]

    def fold(results, kb, diagonal, carry, scale, acc):
        for t, (pv, total) in enumerate(results):
            if t > 0 or not diagonal:
                valid = jnp.where(kb - t >= 0, 1.0, 0.0)
                scale, total = scale * valid, total * valid
            acc = acc + scale * pv
            carry = carry + total
            scale = jnp.exp2(carry)
        return carry, scale, acc

    qs = [pl.program_id(2) * n_sub + sub for sub in range(n_sub)]
    q = [q_ref[sub * ts:(sub + 1) * ts, :] for sub in range(n_sub)]

    def advance(kbs, count, diagonal, states):
        results = weights_times_v([spec for sub in range(n_sub)
                                   for spec in tile_specs(q[sub], kbs[sub], count, diagonal)])
        live, new_states = 0.0, []
        for sub in range(n_sub):
            state = fold(results[sub * count:(sub + 1) * count], kbs[sub], diagonal, *states[sub])
            live = jnp.maximum(live, jnp.where(kbs[sub] - count >= 0, jnp.max(state[1]), 0.0))
            new_states.append(state)
        return live, tuple(new_states)

    zeros = jnp.zeros((ts, HEAD_DIM), F32)
    live, states = advance(qs, n_first, True, [(zeros, jnp.ones((ts, HEAD_DIM), F32), zeros)] * n_sub)

    def body(loop_state):
        it, _, states = loop_state
        kbs = [qs[sub] - n_first - PAIR * it for sub in range(n_sub)]
        return (it + 1,) + advance(kbs, PAIR, False, states)

    states = lax.while_loop(lambda loop_state: loop_state[1] > 0.0, body, (0, live, states))[2]
    for sub in range(n_sub):
        o_ref[sub * ts:(sub + 1) * ts, :] = states[sub][2].astype(o_ref.dtype)


def _stick_breaking(proj, batch, seq, tq, ts):
    m = proj.shape[0]
    nq = seq // tq
    return pl.pallas_call(
        functools.partial(_sb_kernel, ts=ts, n_sub=tq // ts, n_first=3),
        grid=(batch, N_HEADS, nq),
        in_specs=[
            pl.BlockSpec((tq, HEAD_DIM), lambda b, h, i: (b * nq + i, h)),
            pl.BlockSpec((seq, HEAD_DIM), lambda b, h, i: (b, N_HEADS + h)),
            pl.BlockSpec((seq, HEAD_DIM), lambda b, h, i: (b, 2 * N_HEADS + h)),
        ],
        out_specs=pl.BlockSpec((tq, HEAD_DIM), lambda b, h, i: (b * nq + i, h)),
        out_shape=jax.ShapeDtypeStruct((m, N_HEADS * HEAD_DIM), BF16),
        scratch_shapes=[pltpu.VMEM((2 * ts, ts + LANES), BF16), pltpu.VMEM((ts, ts), F32)],
        compiler_params=_params("arbitrary", "arbitrary", "arbitrary"),
        name="stick_breaking",
    )(proj, proj, proj)


def _ret_kernel(lg_ref, q_ref, k_ref, v_ref, gate_ref, gn_ref, o_ref, state_ref, entering_ref,
                *, chunk, n_chunks, scale):
    @pl.when(pl.program_id(2) == 0)
    def _():
        state_ref[...] = jnp.zeros_like(state_ref)

    lg = lg_ref[pl.program_id(1)]
    c = chunk
    row = lax.broadcasted_iota(jnp.int32, (c, c), 0).astype(F32)
    col = lax.broadcasted_iota(jnp.int32, (c, c), 1).astype(F32)
    diff = row - col
    decay = jnp.where(diff >= 0, jnp.exp(lg * jnp.maximum(diff, 0.0)), 0.0) * scale
    pos = lax.broadcasted_iota(jnp.int32, (c, HEAD_DIM), 0).astype(F32)
    q_decay = jnp.exp(lg * (pos + 1.0))
    k_decay = jnp.exp(lg * (c - 1.0 - pos)) * scale
    chunk_decay = jnp.exp(jnp.zeros((HEAD_DIM, HEAD_DIM), F32) + lg * c)
    gn = gn_ref[...]

    state = state_ref[...]
    for n in range(n_chunks):
        rows = slice(n * c, (n + 1) * c)
        entering_ref[n] = state.astype(BF16)
        kv = lax.dot_general((k_ref[rows, :].astype(F32) * k_decay).astype(BF16), v_ref[rows, :],
                             (((0,), (0,)), ((), ())), preferred_element_type=F32)
        state = chunk_decay * state + kv
    state_ref[...] = state

    for n in range(n_chunks):
        rows = slice(n * c, (n + 1) * c)
        q = q_ref[rows, :]
        scores = lax.dot_general(q, k_ref[rows, :], (((1,), (1,)), ((), ())),
                                 preferred_element_type=F32) * decay
        out = _bdot(scores.astype(BF16), v_ref[rows, :]) + q_decay * _bdot(q, entering_ref[n])
        mu = jnp.mean(out, axis=-1, keepdims=True)
        cen = out - mu
        var = jnp.mean(cen * cen, axis=-1, keepdims=True)
        normed = cen * lax.rsqrt(var + EPS) * gn
        g = gate_ref[rows, :].astype(F32)
        o_ref[rows, :] = (g * jax.nn.sigmoid(g) * normed).astype(o_ref.dtype)


def _retention(proj, head0, ret_gn, layer, batch, seq, rows_per_step):
    m = proj.shape[0]
    ns = seq // rows_per_step
    log_gamma = jnp.log1p(-jnp.exp2(-5.0 - jnp.arange(N_HEADS, dtype=F32)))
    blk = (rows_per_step, HEAD_DIM)

    def part(p):
        return pl.BlockSpec(blk, lambda b, h, s: (b * ns + s, head0 + p * N_HEADS + h))

    return pl.pallas_call(
        functools.partial(_ret_kernel, chunk=RET_CHUNK, n_chunks=rows_per_step // RET_CHUNK,
                          scale=HEAD_DIM ** -0.5),
        grid=(batch, N_HEADS, ns),
        in_specs=[
            pl.BlockSpec(memory_space=pltpu.SMEM),
            part(0), part(1), part(2), part(3),
            pl.BlockSpec((None, 1, HEAD_DIM), lambda b, h, s: (layer * N_HEADS + h, 0, 0)),
        ],
        out_specs=pl.BlockSpec(blk, lambda b, h, s: (b * ns + s, h)),
        out_shape=jax.ShapeDtypeStruct((m, N_HEADS * HEAD_DIM), BF16),
        scratch_shapes=[pltpu.VMEM((HEAD_DIM, HEAD_DIM), F32),
                        pltpu.VMEM((rows_per_step // RET_CHUNK, HEAD_DIM, HEAD_DIM), BF16)],
        compiler_params=_params("arbitrary", "arbitrary", "arbitrary"),
        name="retention",
    )(log_gamma, proj, proj, proj, proj, ret_gn.reshape(-1, 1, HEAD_DIM))


def _rope_tables(seq):
    half = HEAD_DIM // 2
    inv_freq = ROPE_BASE ** (-jnp.arange(half, dtype=F32) / half)
    ang = jnp.arange(seq, dtype=jnp.int32).astype(F32)[:, None] * inv_freq[None, :]
    cos, sin = jnp.cos(ang), jnp.sin(ang)
    return (jnp.concatenate([cos, cos], axis=-1), jnp.concatenate([-sin, sin], axis=-1))


def kernel(x, c, ada_down, ada_up, ada_bias, norm_ffn1, ffn1_in, ffn1_out, norm_mix, w_in, ret_gn,
           w_branch_a, w_branch_b, w_out, norm_ffn2, ffn2_in, ffn2_out, norm_final):
    batch, seq, d = x.shape
    depth = ada_down.shape[0]
    width = N_HEADS * HEAD_DIM
    tm_wide = min(2048, seq)
    tm_deep = min(512, seq)
    tq = min(1024, seq)
    tn = 512
    rope = _rope_tables(seq)
    table = _ada_modulation(c, ada_down, ada_up, ada_bias)
    xf = x.reshape(batch * seq, d)
    for l in range(depth):
        mod = _Mod(table, l, batch, seq)
        h = _rms_norm(xf, norm_ffn1, l, seq, mod, 0, 1)
        act, w_bf16 = _ffn_in(h, ffn1_in, ffn1_out, l, tm_wide)
        xf = _resid_matmul(act, w_bf16, 0, xf, mod, 2, 0.5, tm_deep, tn, resident_panel=False)
        h = _rms_norm(xf, norm_mix, l, seq, mod, 3, 4)
        proj = _in_proj(h, w_in, l, seq, tm_wide, tn, rope, width, HEAD_DIM ** -0.5 * LOG2_E)
        attn = _stick_breaking(proj, batch, seq, tq, HEAD_DIM)
        retg = _retention(proj, 3 * N_HEADS, ret_gn, l, batch, seq, min(2048, seq))
        merged = _branch_merge(attn, retg, w_branch_a, w_branch_b, l, proj, 7 * width, tm_wide, tn)
        xf = _resid_matmul(merged, w_out, l, xf, mod, 5, 1.0, tm_wide, tn)
        h = _rms_norm(xf, norm_ffn2, l, seq, mod, 6, 7)
        act, w_bf16 = _ffn_in(h, ffn2_in, ffn2_out, l, tm_wide)
        xf = _resid_matmul(act, w_bf16, 0, xf, mod, 8, 0.5, tm_deep, tn, resident_panel=False)
    out = _rms_norm(xf, norm_final.reshape(1, d), 0, seq, out_dtype=x.dtype)
    return out.reshape(batch, seq, d)
```
